```python
import jax, jax.numpy as jnp
from jax import lax
import numpy as np

D_MODEL = 1024
BATCH = 8
SEQ = 2048
DEPTH = 1
DEC_BATCH = 2
DEC_SEQ = 8192
PAST_LEN = 128

N_META = 16
BLOCK = 128
WINDOW = 128
LEAD_PAD = BLOCK - N_META
ROPE_THETA = 10000.0
EPS = 1e-6
NEG = -1e30
MLA_HEADS = 8
MLA_Q_LORA = 384
MLA_KV_LORA = 256
MLA_NOPE = 64
MLA_ROPE = 32
MLA_V = 64
SWA_HEADS = 8
SWA_KV_HEADS = 2
SWA_HEAD_DIM = 64
SWA_GROUP = SWA_HEADS // SWA_KV_HEADS
D_FF = 2816
COL_SIZES = (MLA_Q_LORA, MLA_KV_LORA, MLA_ROPE,
             SWA_HEADS * SWA_HEAD_DIM, SWA_KV_HEADS * SWA_HEAD_DIM, SWA_KV_HEADS * SWA_HEAD_DIM,
             D_MODEL, D_MODEL)
D_IN = sum(COL_SIZES)

kernel_name = "hybrid_mla_window_gqa_macaron_encoder"


def rms_norm(x, g):
    xf = x.astype(jnp.float32)
    y = xf * lax.rsqrt(jnp.mean(xf * xf, axis=-1, keepdims=True) + EPS)
    return (y * g.astype(jnp.float32)).astype(x.dtype)


def rope(x, pos):
    half = x.shape[-1] // 2
    inv = ROPE_THETA ** (-jnp.arange(half, dtype=jnp.float32) / half)
    ang = pos.astype(jnp.float32)[:, None] * inv[None, :]
    cos = jnp.cos(ang)[:, None, :]
    sin = jnp.sin(ang)[:, None, :]
    xf = x.astype(jnp.float32)
    x1, x2 = xf[..., :half], xf[..., half:]
    return jnp.concatenate([x1 * cos - x2 * sin, x2 * cos + x1 * sin], axis=-1).astype(x.dtype)


def swiglu(x, w_in, w_out):
    a = x @ w_in
    g, u = a[..., :D_FF], a[..., D_FF:]
    return (jax.nn.silu(g) * u) @ w_out


def mla(c_q, c_kv, k_rope, pos, key_ok, q_norm_g, w_uq, kv_norm_g, w_ukv):
    B, L, _ = c_q.shape
    q = (rms_norm(c_q, q_norm_g) @ w_uq).reshape(B, L, MLA_HEADS, MLA_NOPE + MLA_ROPE)
    q_nope = q[..., :MLA_NOPE]
    q_rope = rope(q[..., MLA_NOPE:], pos)
    kv = (rms_norm(c_kv, kv_norm_g) @ w_ukv).reshape(B, L, MLA_HEADS, MLA_NOPE + MLA_V)
    k_nope = kv[..., :MLA_NOPE]
    v = kv[..., MLA_NOPE:]
    k_r = rope(k_rope[:, :, None, :], pos)[:, :, 0, :]
    scale = (MLA_NOPE + MLA_ROPE) ** -0.5
    bias = jnp.where(key_ok, 0.0, NEG).astype(jnp.float32)
    nb = L // BLOCK
    qn_b = q_nope.reshape(B, nb, BLOCK, MLA_HEADS, MLA_NOPE).transpose(1, 0, 2, 3, 4)
    qr_b = q_rope.reshape(B, nb, BLOCK, MLA_HEADS, MLA_ROPE).transpose(1, 0, 2, 3, 4)

    def one_block(args):
        qn, qr = args
        s = (jnp.einsum('bqhd,bkhd->bhqk', qn, k_nope).astype(jnp.float32)
             + jnp.einsum('bqhr,bkr->bhqk', qr, k_r).astype(jnp.float32)) * scale + bias
        p = jax.nn.softmax(s, axis=-1)
        return jnp.einsum('bhqk,bkhd->bqhd', p.astype(v.dtype), v)

    o = lax.map(one_block, (qn_b, qr_b))
    return o.transpose(1, 0, 2, 3, 4).reshape(B, L, MLA_HEADS * MLA_V)


def window_gqa(q, k, v, pos, sink):
    B, L, _ = q.shape
    nb = L // BLOCK
    q = rope(q.reshape(B, L, SWA_HEADS, SWA_HEAD_DIM), pos)
    q = q.reshape(B, nb, BLOCK, SWA_KV_HEADS, SWA_GROUP, SWA_HEAD_DIM)
    k = rope(k.reshape(B, L, SWA_KV_HEADS, SWA_HEAD_DIM), pos)
    v = v.reshape(B, L, SWA_KV_HEADS, SWA_HEAD_DIM)
    k_meta = k[:, LEAD_PAD:BLOCK]
    v_meta = v[:, LEAD_PAD:BLOCK]

    def band(t):
        tb = t.reshape(B, nb, BLOCK, SWA_KV_HEADS, SWA_HEAD_DIM)
        tp = jnp.pad(tb, ((0, 0), (1, 1), (0, 0), (0, 0), (0, 0)))
        return jnp.concatenate([tp[:, :-2], tp[:, 1:-1], tp[:, 2:]], axis=2)

    kb, vb = band(k), band(v)
    qi = jnp.arange(L).reshape(nb, BLOCK)
    ki = (jnp.arange(nb)[:, None] - 1) * BLOCK + jnp.arange(3 * BLOCK)[None, :]
    ok = ((ki[:, None, :] >= BLOCK) & (ki[:, None, :] < L)
          & (jnp.abs(qi[:, :, None] - ki[:, None, :]) <= WINDOW))
    scale = SWA_HEAD_DIM ** -0.5
    s_band = jnp.einsum('bnqhgd,bnchd->bnhgqc', q, kb).astype(jnp.float32) * scale
    s_band = jnp.where(ok[None, :, None, None, :, :], s_band, NEG)
    s_meta = jnp.einsum('bnqhgd,bmhd->bnhgqm', q, k_meta).astype(jnp.float32) * scale
    s_sink = jnp.broadcast_to(
        sink.astype(jnp.float32).reshape(SWA_KV_HEADS, SWA_GROUP)[None, None, :, :, None, None],
        s_meta.shape[:-1] + (1,))
    p = jax.nn.softmax(jnp.concatenate([s_band, s_meta, s_sink], axis=-1), axis=-1)
    p_band = p[..., :3 * BLOCK].astype(v.dtype)
    p_meta = p[..., 3 * BLOCK:3 * BLOCK + N_META].astype(v.dtype)
    o = (jnp.einsum('bnhgqc,bnchd->bnqhgd', p_band, vb)
         + jnp.einsum('bnhgqm,bmhd->bnqhgd', p_meta, v_meta))
    return o.reshape(B, L, SWA_HEADS * SWA_HEAD_DIM)


def encoder_layer(x, pos, key_ok,
                  ffn1_pre_g, ffn1_w_in, ffn1_w_out, ffn1_post_g,
                  mix_pre_g, w_in, q_norm_g, w_uq, kv_norm_g, w_ukv, sink,
                  w_o_a, w_o_b, w_out, mix_post_g,
                  ffn2_pre_g, ffn2_w_in, ffn2_w_out, ffn2_post_g):
    x = x + 0.5 * rms_norm(swiglu(rms_norm(x, ffn1_pre_g), ffn1_w_in, ffn1_w_out), ffn1_post_g)
    h = rms_norm(x, mix_pre_g)
    proj = h @ w_in
    splits = np.cumsum(COL_SIZES)[:-1].tolist()
    c_q, c_kv, k_r, q_s, k_s, v_s, g_a, g_b = jnp.split(proj, splits, axis=-1)
    y_a = mla(c_q, c_kv, k_r, pos, key_ok, q_norm_g, w_uq, kv_norm_g, w_ukv) @ w_o_a
    y_b = window_gqa(q_s, k_s, v_s, pos, sink) @ w_o_b
    merged = jax.nn.sigmoid(g_a) * y_a + jax.nn.sigmoid(g_b) * y_b
    x = x + rms_norm(merged @ w_out, mix_post_g)
    x = x + 0.5 * rms_norm(swiglu(rms_norm(x, ffn2_pre_g), ffn2_w_in, ffn2_w_out), ffn2_post_g)
    return x


def run_trunk(x, meta_tokens, weights):
    B, S, D = x.shape
    lead = jnp.concatenate([jnp.zeros((LEAD_PAD, D), x.dtype), meta_tokens.astype(x.dtype)], axis=0)
    h = jnp.concatenate([jnp.broadcast_to(lead[None], (B, BLOCK, D)), x], axis=1)
    L = S + BLOCK
    pos = jnp.arange(L) - LEAD_PAD
    key_ok = pos >= 0
    for l in range(DEPTH):
        h = encoder_layer(h, pos, key_ok, *[w[l] for w in weights])
    return h[:, BLOCK:]


def setup_inputs(seed: int = 0) -> dict:
    key = jax.random.key(seed)
    ks = jax.random.split(key, 24)

    def w(k, shape, fan_in):
        return jax.random.normal(k, shape, jnp.float32) * fan_in ** -0.5

    def gain(k, n):
        return 1.0 + 0.01 * jax.random.normal(k, (DEPTH, n), jnp.float32)

    return {
        "x_prompt": jax.random.normal(ks[0], (BATCH, SEQ, D_MODEL), jnp.float32),
        "x_sample": jax.random.normal(ks[1], (DEC_BATCH, DEC_SEQ, D_MODEL), jnp.float32),
        "meta_tokens": jax.random.normal(ks[2], (N_META, D_MODEL), jnp.float32),
        "ffn1_pre_g": gain(ks[3], D_MODEL),
        "ffn1_w_in": w(ks[4], (DEPTH, D_MODEL, 2 * D_FF), D_MODEL),
        "ffn1_w_out": w(ks[5], (DEPTH, D_FF, D_MODEL), D_FF),
        "ffn1_post_g": gain(ks[6], D_MODEL),
        "mix_pre_g": gain(ks[7], D_MODEL),
        "w_in": w(ks[8], (DEPTH, D_MODEL, D_IN), D_MODEL),
        "q_norm_g": gain(ks[9], MLA_Q_LORA),
        "w_uq": w(ks[10], (DEPTH, MLA_Q_LORA, MLA_HEADS * (MLA_NOPE + MLA_ROPE)), MLA_Q_LORA),
        "kv_norm_g": gain(ks[11], MLA_KV_LORA),
        "w_ukv": w(ks[12], (DEPTH, MLA_KV_LORA, MLA_HEADS * (MLA_NOPE + MLA_V)), MLA_KV_LORA),
        "sink": 0.5 * jax.random.normal(ks[13], (DEPTH, SWA_HEADS), jnp.float32),
        "w_o_a": w(ks[14], (DEPTH, MLA_HEADS * MLA_V, D_MODEL), MLA_HEADS * MLA_V),
        "w_o_b": w(ks[15], (DEPTH, SWA_HEADS * SWA_HEAD_DIM, D_MODEL), SWA_HEADS * SWA_HEAD_DIM),
        "w_out": w(ks[16], (DEPTH, D_MODEL, D_MODEL), D_MODEL),
        "mix_post_g": gain(ks[17], D_MODEL),
        "ffn2_pre_g": gain(ks[18], D_MODEL),
        "ffn2_w_in": w(ks[19], (DEPTH, D_MODEL, 2 * D_FF), D_MODEL),
        "ffn2_w_out": w(ks[20], (DEPTH, D_FF, D_MODEL), D_FF),
        "ffn2_post_g": gain(ks[21], D_MODEL),
    }


def reference(x_prompt, x_sample, meta_tokens,
              ffn1_pre_g, ffn1_w_in, ffn1_w_out, ffn1_post_g,
              mix_pre_g, w_in, q_norm_g, w_uq, kv_norm_g, w_ukv, sink,
              w_o_a, w_o_b, w_out, mix_post_g,
              ffn2_pre_g, ffn2_w_in, ffn2_w_out, ffn2_post_g):
    weights = (ffn1_pre_g, ffn1_w_in, ffn1_w_out, ffn1_post_g,
               mix_pre_g, w_in, q_norm_g, w_uq, kv_norm_g, w_ukv, sink,
               w_o_a, w_o_b, w_out, mix_post_g,
               ffn2_pre_g, ffn2_w_in, ffn2_w_out, ffn2_post_g)
    y_prompt = run_trunk(x_prompt, meta_tokens, weights)
    y_sample = run_trunk(x_sample, meta_tokens, weights)
    return (y_prompt, y_sample)
```

```python
import functools

import jax
import jax.numpy as jnp
from jax import lax
from jax.experimental import pallas as pl
from jax.experimental.pallas import tpu as pltpu

D_MODEL = 1024
N_META = 16
BLOCK = 128
WINDOW = 128
LEAD_PAD = BLOCK - N_META
ROPE_THETA = 10000.0
EPS = 1e-6
NEG = -1e30
MLA_HEADS = 8
MLA_Q_LORA = 384
MLA_KV_LORA = 256
MLA_NOPE = 64
MLA_ROPE = 32
MLA_V = 64
SWA_HEADS = 8
SWA_KV_HEADS = 2
SWA_HEAD_DIM = 64
SWA_GROUP = SWA_HEADS // SWA_KV_HEADS
D_FF = 2816

LANES = 128
MLA_SCALE = (MLA_NOPE + MLA_ROPE) ** -0.5
SWA_SCALE = SWA_HEAD_DIM ** -0.5
V7X_VMEM_BYTES = 64 * 1024 * 1024
VMEM_LIMIT = (V7X_VMEM_BYTES * 7) // 8

_C_Q = (0, MLA_Q_LORA)
_C_KV = (_C_Q[1], _C_Q[1] + MLA_KV_LORA)
_C_QS = (_C_KV[1], _C_KV[1] + SWA_HEADS * SWA_HEAD_DIM)
_C_QS_ROT = (_C_QS[1], _C_QS[1] + SWA_HEADS * SWA_HEAD_DIM)
_C_KS = (_C_QS_ROT[1], _C_QS_ROT[1] + LANES)
_C_KS_ROT = (_C_KS[1], _C_KS[1] + LANES)
_C_VS = (_C_KS_ROT[1], _C_KS_ROT[1] + LANES)
_C_KR = (_C_VS[1], _C_VS[1] + LANES)
_C_KR_ROT = (_C_KR[1], _C_KR[1] + LANES)
D_PROJ = _C_KR_ROT[1]

FFN_CHUNK = D_FF // 2


def _const_spec(shape):
    zeros = (0,) * len(shape)
    return pl.BlockSpec(shape, lambda *_: zeros, pipeline_mode=pl.Buffered(1))


def _params(n_axes):
    return pltpu.CompilerParams(
        dimension_semantics=("parallel",) * n_axes, vmem_limit_bytes=VMEM_LIMIT)


def _rms(x, g):
    y = x * lax.rsqrt(jnp.mean(x * x, axis=-1, keepdims=True) + EPS)
    return y * g


def _dot(a, b):
    return jnp.dot(a, b, preferred_element_type=jnp.float32)


def _dot_nt(a, b):
    return lax.dot_general(a, b, (((1,), (1,)), ((), ())),
                           preferred_element_type=jnp.float32)


def _swiglu_residual(x, pre_g, w_in_ref, w_out_ref, post_g):
    h = _rms(x, pre_g).astype(jnp.bfloat16)
    acc = None
    for c in range(D_FF // FFN_CHUNK):
        lo = c * FFN_CHUNK
        g = _dot(h, w_in_ref[:, lo:lo + FFN_CHUNK])
        u = _dot(h, w_in_ref[:, D_FF + lo:D_FF + lo + FFN_CHUNK])
        a = (g * jax.nn.sigmoid(g) * u).astype(jnp.bfloat16)
        part = _dot(a, w_out_ref[lo:lo + FFN_CHUNK, :])
        acc = part if acc is None else acc + part
    return x + 0.5 * _rms(acc, post_g)


def _ffn_kernel(x_ref, pre_g_ref, w_in_ref, w_out_ref, post_g_ref, o_ref):
    o_ref[0] = _swiglu_residual(x_ref[0], pre_g_ref[...], w_in_ref, w_out_ref,
                                post_g_ref[...])


def _ffn(x, pre_g, w_in, w_out, post_g, tm):
    b, s, d = x.shape
    row = pl.BlockSpec((1, tm, d), lambda i, t: (i, t, 0))
    return pl.pallas_call(
        _ffn_kernel,
        grid=(b, s // tm),
        in_specs=[row, _const_spec(pre_g.shape), _const_spec(w_in.shape),
                  _const_spec(w_out.shape), _const_spec(post_g.shape)],
        out_specs=row,
        out_shape=jax.ShapeDtypeStruct(x.shape, jnp.float32),
        compiler_params=_params(2),
        name="ffn",
    )(x, pre_g, w_in, w_out, post_g)


def _proj_kernel(x_ref, tab_ref, pre_g_ref, w_ref, qn_g_ref, wq_ref, kvn_g_ref, wkv_ref,
                 qa_ref, ka_ref, va_ref, qw_ref, kw_ref, vw_ref):
    bf16 = jnp.bfloat16
    h = _rms(x_ref[0], pre_g_ref[...]).astype(bf16)
    tab = tab_ref[...]
    cq_t, sq_t, ck_t, sk_t, cw_t, sw_t = [tab[:, i * LANES:(i + 1) * LANES] for i in range(6)]
    lane = lax.broadcasted_iota(jnp.int32, (1, LANES), 1)
    low_half = lane < SWA_HEAD_DIM

    def cols(rng):
        return _dot(h, w_ref[:, rng[0]:rng[1]])

    qn = _rms(cols(_C_Q), qn_g_ref[...]).astype(bf16)
    qq = _dot(qn, wq_ref[...])
    n_q = MLA_HEADS * LANES
    for hd in range(MLA_HEADS):
        sl = slice(hd * LANES, (hd + 1) * LANES)
        qa_ref[0, hd] = (qq[:, sl] * cq_t + qq[:, n_q + hd * LANES:n_q + (hd + 1) * LANES] * sq_t
                         ).astype(bf16)

    kvn = _rms(cols(_C_KV), kvn_g_ref[...]).astype(bf16)
    kk = _dot(kvn, wkv_ref[...])
    k_rope = cols(_C_KR) * ck_t + cols(_C_KR_ROT) * sk_t
    for hd in range(MLA_HEADS):
        sl = slice(hd * LANES, (hd + 1) * LANES)
        ka_ref[0, hd] = (kk[:, sl] + k_rope).astype(bf16)
        va_ref[0, hd] = kk[:, n_q + hd * LANES:n_q + (hd + 1) * LANES].astype(bf16)

    qs = cols(_C_QS)
    qs_rot = cols(_C_QS_ROT)
    for pair in range(SWA_HEADS // 2):
        sl = slice(pair * LANES, (pair + 1) * LANES)
        q_pair = (qs[:, sl] * cw_t + qs_rot[:, sl] * sw_t) * SWA_SCALE
        qw_ref[0, 2 * pair] = jnp.where(low_half, q_pair, 0.0).astype(bf16)
        qw_ref[0, 2 * pair + 1] = jnp.where(low_half, 0.0, q_pair).astype(bf16)

    k_pair = cols(_C_KS) * cw_t + cols(_C_KS_ROT) * sw_t
    v_pair = cols(_C_VS)
    for src, dst in ((k_pair, kw_ref), (v_pair, vw_ref)):
        swapped = pltpu.roll(src, SWA_HEAD_DIM, 1)
        dst[0, 0] = jnp.where(low_half, src, swapped).astype(bf16)
        dst[0, 1] = jnp.where(low_half, swapped, src).astype(bf16)


def _proj(x1, tab, pre_g, w_proj, qn_g, wq, kvn_g, wkv, tm):
    b, s, d = x1.shape
    row = pl.BlockSpec((1, tm, d), lambda i, t: (i, t, 0))
    tab_spec = pl.BlockSpec((tm, tab.shape[1]), lambda i, t: (t, 0))

    def heads(n):
        return pl.BlockSpec((1, n, tm, LANES), lambda i, t: (i, 0, t, 0))

    def out(n):
        return jax.ShapeDtypeStruct((b, n, s, LANES), jnp.bfloat16)

    return pl.pallas_call(
        _proj_kernel,
        grid=(b, s // tm),
        in_specs=[row, tab_spec, _const_spec(pre_g.shape), _const_spec(w_proj.shape),
                  _const_spec(qn_g.shape), _const_spec(wq.shape),
                  _const_spec(kvn_g.shape), _const_spec(wkv.shape)],
        out_specs=[heads(MLA_HEADS), heads(MLA_HEADS), heads(MLA_HEADS),
                   heads(SWA_HEADS), heads(SWA_KV_HEADS), heads(SWA_KV_HEADS)],
        out_shape=[out(MLA_HEADS), out(MLA_HEADS), out(MLA_HEADS),
                   out(SWA_HEADS), out(SWA_KV_HEADS), out(SWA_KV_HEADS)],
        compiler_params=_params(2),
        name="proj",
    )(x1, tab, pre_g, w_proj, qn_g, wq, kvn_g, wkv)


MLA_HEADS_PER_STEP = 2


def _mla_kernel(q_ref, k_ref, v_ref, kl_ref, vl_ref, o_ref, *, tk):
    n_chunks = k_ref.shape[2] // tk
    lane = lax.broadcasted_iota(jnp.int32, (1, BLOCK), 1)
    lead_ok = lane >= LEAD_PAD
    out = None
    for j in range(MLA_HEADS_PER_STEP):
        q = q_ref[0, j]
        s = jnp.where(lead_ok, _dot_nt(q, kl_ref[0, j]), NEG)
        m = jnp.max(s, axis=-1, keepdims=True)
        p = jnp.exp(s - m)
        l = jnp.sum(p, axis=-1, keepdims=True)
        acc = _dot(p.astype(jnp.bfloat16), vl_ref[0, j])

        def body(c, carry, j=j, q=q):
            m, l, acc = carry
            start = pl.multiple_of(c * tk, tk)
            s = _dot_nt(q, k_ref[0, j, pl.ds(start, tk), :])
            m_new = jnp.maximum(m, jnp.max(s, axis=-1, keepdims=True))
            alpha = jnp.exp(m - m_new)
            p = jnp.exp(s - m_new)
            l = alpha * l + jnp.sum(p, axis=-1, keepdims=True)
            acc = alpha * acc + _dot(p.astype(jnp.bfloat16), v_ref[0, j, pl.ds(start, tk), :])
            return m_new, l, acc

        m, l, acc = lax.fori_loop(0, n_chunks, body, (m, l, acc))
        o = acc * (1.0 / l)
        out = o if out is None else out + o
    o_ref[0] = out.astype(o_ref.dtype)


def _mla(q, k, v, k_lead, v_lead, tq, tk):
    b, nh, s, _ = q.shape
    hp = MLA_HEADS_PER_STEP
    q_spec = pl.BlockSpec((1, hp, tq, LANES), lambda i, p, t: (i, p, t, 0))
    kv_spec = pl.BlockSpec((1, hp, s, LANES), lambda i, p, t: (i, p, 0, 0))
    lead_spec = pl.BlockSpec((1, hp, BLOCK, LANES), lambda i, p, t: (0, p, 0, 0))
    return pl.pallas_call(
        functools.partial(_mla_kernel, tk=tk),
        grid=(b, nh // hp, s // tq),
        in_specs=[q_spec, kv_spec, kv_spec, lead_spec, lead_spec],
        out_specs=pl.BlockSpec((1, tq, LANES), lambda i, p, t: (i, t, p)),
        out_shape=jax.ShapeDtypeStruct((b, s, nh * MLA_V), jnp.bfloat16),
        compiler_params=_params(3),
        name="mla",
    )(q, k, v, k_lead, v_lead)


def _window_kernel(sink_ref, q_ref, kl_ref, kc_ref, kr_ref, km_ref,
                   vl_ref, vc_ref, vr_ref, vm_ref, o_ref):
    tq = q_ref.shape[2]
    t = pl.program_id(1)
    n_t = pl.num_programs(1)
    row = lax.broadcasted_iota(jnp.int32, (tq, BLOCK), 0)
    col = lax.broadcasted_iota(jnp.int32, (tq, BLOCK), 1)
    ok_left = jnp.logical_and(row - (col - BLOCK) <= WINDOW, t > 0)
    ok_right = jnp.logical_and((tq + col) - row <= WINDOW, t < n_t - 1)
    rowc = lax.broadcasted_iota(jnp.int32, (tq, tq), 0)
    colc = lax.broadcasted_iota(jnp.int32, (tq, tq), 1)
    ok_center = jnp.abs(rowc - colc) <= WINDOW
    ok_meta = col >= LEAD_PAD
    low_half = lax.broadcasted_iota(jnp.int32, (1, LANES), 1) < SWA_HEAD_DIM

    for g in range(SWA_KV_HEADS):
        q = q_ref[0, g * SWA_GROUP:(g + 1) * SWA_GROUP].reshape(SWA_GROUP * tq, LANES)
        pieces = []
        for k_ref, v_ref, ok in ((kl_ref, vl_ref, ok_left), (kc_ref, vc_ref, ok_center),
                                 (kr_ref, vr_ref, ok_right), (km_ref, vm_ref, ok_meta)):
            s = _dot_nt(q, k_ref[0, g]).reshape(SWA_GROUP, tq, -1)
            pieces.append((jnp.where(ok[None], s, NEG), v_ref))
        sink = jnp.concatenate(
            [jnp.full((1, tq, 1), sink_ref[g * SWA_GROUP + r], jnp.float32)
             for r in range(SWA_GROUP)], axis=0)
        m = sink
        for s, _ in pieces:
            m = jnp.maximum(m, jnp.max(s, axis=-1, keepdims=True))
        l = jnp.exp(sink - m)
        acc = None
        for s, v_ref in pieces:
            p = jnp.exp(s - m)
            l = l + jnp.sum(p, axis=-1, keepdims=True)
            pv = _dot(p.astype(jnp.bfloat16).reshape(SWA_GROUP * tq, -1), v_ref[0, g])
            acc = pv if acc is None else acc + pv
        o = acc.reshape(SWA_GROUP, tq, LANES) * (1.0 / l)
        for pair in range(SWA_GROUP // 2):
            slab = g * (SWA_GROUP // 2) + pair
            o_ref[0, :, slab * LANES:(slab + 1) * LANES] = jnp.where(
                low_half, o[2 * pair], o[2 * pair + 1]).astype(o_ref.dtype)


def _window(sink, q, k, v, k_lead, v_lead, tq):
    b, nh, s, _ = q.shape
    n_blk = s // BLOCK
    per = tq // BLOCK
    q_spec = pl.BlockSpec((1, nh, tq, LANES), lambda i, t: (i, 0, t, 0))
    left = pl.BlockSpec((1, SWA_KV_HEADS, BLOCK, LANES),
                        lambda i, t: (i, 0, jnp.maximum(t * per - 1, 0), 0))
    center = pl.BlockSpec((1, SWA_KV_HEADS, tq, LANES), lambda i, t: (i, 0, t, 0))
    right = pl.BlockSpec((1, SWA_KV_HEADS, BLOCK, LANES),
                         lambda i, t: (i, 0, jnp.minimum((t + 1) * per, n_blk - 1), 0))
    lead = pl.BlockSpec((1, SWA_KV_HEADS, BLOCK, LANES), lambda i, t: (0, 0, 0, 0))
    smem = pl.BlockSpec(memory_space=pltpu.SMEM)
    return pl.pallas_call(
        _window_kernel,
        grid=(b, s // tq),
        in_specs=[smem, q_spec, left, center, right, lead, left, center, right, lead],
        out_specs=pl.BlockSpec((1, tq, nh * SWA_HEAD_DIM), lambda i, t: (i, t, 0)),
        out_shape=jax.ShapeDtypeStruct((b, s, nh * SWA_HEAD_DIM), jnp.bfloat16),
        compiler_params=_params(2),
        name="window",
    )(sink, q, k, k, k, k_lead, v, v, v, v_lead)


def _mixout_kernel(x_ref, oa_ref, ob_ref, pre_g_ref, wg_ref, woa_ref, wob_ref, wout_ref,
                   post_g_ref, o_ref):
    x1 = x_ref[0]
    h = _rms(x1, pre_g_ref[...]).astype(jnp.bfloat16)
    y_a = _dot(oa_ref[0], woa_ref[...])
    y_b = _dot(ob_ref[0], wob_ref[...])
    merged = (jax.nn.sigmoid(_dot(h, wg_ref[:, :D_MODEL])) * y_a
              + jax.nn.sigmoid(_dot(h, wg_ref[:, D_MODEL:])) * y_b)
    o_ref[0] = x1 + _rms(_dot(merged.astype(jnp.bfloat16), wout_ref[...]), post_g_ref[...])


def _mixout(x1, o_a, o_b, pre_g, wg, woa, wob, wout, post_g, tm):
    b, s, d = x1.shape
    row = pl.BlockSpec((1, tm, d), lambda i, t: (i, t, 0))
    half = pl.BlockSpec((1, tm, o_a.shape[2]), lambda i, t: (i, t, 0))
    return pl.pallas_call(
        _mixout_kernel,
        grid=(b, s // tm),
        in_specs=[row, half, half, _const_spec(pre_g.shape), _const_spec(wg.shape),
                  _const_spec(woa.shape), _const_spec(wob.shape), _const_spec(wout.shape),
                  _const_spec(post_g.shape)],
        out_specs=row,
        out_shape=jax.ShapeDtypeStruct(x1.shape, jnp.float32),
        compiler_params=_params(2),
        name="mixout",
    )(x1, o_a, o_b, pre_g, wg, woa, wob, wout, post_g)


def _rot_cols(w, n_heads, dim):
    w = w.reshape(w.shape[0], n_heads, dim)
    half = dim // 2
    return jnp.concatenate([-w[..., half:], w[..., :half]], axis=-1).reshape(w.shape[0], -1)


def _pack_weights(w_in, w_uq, w_ukv):
    bf16 = jnp.bfloat16
    d = w_in.shape[0]
    o = 0
    parts = []
    for n in (MLA_Q_LORA, MLA_KV_LORA, MLA_ROPE, SWA_HEADS * SWA_HEAD_DIM,
              SWA_KV_HEADS * SWA_HEAD_DIM, SWA_KV_HEADS * SWA_HEAD_DIM, D_MODEL, D_MODEL):
        parts.append(w_in[:, o:o + n])
        o += n
    w_cq, w_ckv, w_kr, w_qs, w_ks, w_vs, w_ga, w_gb = parts

    def rope_slab(w):
        return jnp.concatenate([jnp.zeros((d, MLA_NOPE), w.dtype), w,
                                jnp.zeros((d, LANES - MLA_NOPE - MLA_ROPE), w.dtype)], axis=1)

    w_proj = jnp.concatenate(
        [w_cq, w_ckv, w_qs, _rot_cols(w_qs, SWA_HEADS, SWA_HEAD_DIM),
         w_ks, _rot_cols(w_ks, SWA_KV_HEADS, SWA_HEAD_DIM), w_vs,
         rope_slab(w_kr), rope_slab(_rot_cols(w_kr, 1, MLA_ROPE))], axis=1).astype(bf16)
    w_gate = jnp.concatenate([w_ga, w_gb], axis=1).astype(bf16)

    r = w_uq.shape[0]
    uq = w_uq.reshape(r, MLA_HEADS, MLA_NOPE + MLA_ROPE)
    pad = jnp.zeros((r, MLA_HEADS, LANES - MLA_NOPE - MLA_ROPE), w_uq.dtype)
    q_plain = jnp.concatenate([uq, pad], axis=-1)
    q_rot = jnp.concatenate(
        [jnp.zeros((r, MLA_HEADS, MLA_NOPE), w_uq.dtype),
         _rot_cols(uq[..., MLA_NOPE:].reshape(r, -1), MLA_HEADS, MLA_ROPE).reshape(
             r, MLA_HEADS, MLA_ROPE), pad], axis=-1)
    wq = jnp.concatenate([q_plain.reshape(r, -1), q_rot.reshape(r, -1)], axis=1).astype(bf16)

    r = w_ukv.shape[0]
    ukv = w_ukv.reshape(r, MLA_HEADS, MLA_NOPE + MLA_V)
    zeros = jnp.zeros((r, MLA_HEADS, LANES - MLA_NOPE), w_ukv.dtype)
    k_slab = jnp.concatenate([ukv[..., :MLA_NOPE], zeros], axis=-1)
    v_h = ukv[..., MLA_NOPE:]
    zv = jnp.zeros_like(v_h)
    odd = (jnp.arange(MLA_HEADS) % 2 == 1)[None, :, None]
    v_slab = jnp.concatenate([jnp.where(odd, zv, v_h), jnp.where(odd, v_h, zv)], axis=-1)
    wkv = jnp.concatenate([k_slab.reshape(r, -1), v_slab.reshape(r, -1)], axis=1).astype(bf16)
    return w_proj, w_gate, wq, wkv


def _rope_tables(pos):
    pos = pos.astype(jnp.float32)[:, None]

    def cs(half):
        inv = ROPE_THETA ** (-jnp.arange(half, dtype=jnp.float32) / half)
        ang = pos * inv[None, :]
        return jnp.cos(ang), jnp.sin(ang)

    n = pos.shape[0]
    c16, s16 = cs(MLA_ROPE // 2)
    c32, s32 = cs(SWA_HEAD_DIM // 2)
    ones = jnp.ones((n, MLA_NOPE), jnp.float32)
    zeros = jnp.zeros((n, MLA_NOPE), jnp.float32)
    tail = jnp.zeros((n, LANES - MLA_NOPE - MLA_ROPE), jnp.float32)
    ck = jnp.concatenate([ones, c16, c16, tail], axis=1)
    sk = jnp.concatenate([zeros, s16, s16, tail], axis=1)
    cw = jnp.concatenate([c32] * 4, axis=1)
    sw = jnp.concatenate([s32] * 4, axis=1)
    return jnp.concatenate([ck * MLA_SCALE, sk * MLA_SCALE, ck, sk, cw, sw], axis=1)


def _row_tile(s):
    return min(s, 512)


def kernel(x_prompt, x_sample, meta_tokens, ffn1_pre_g, ffn1_w_in, ffn1_w_out, ffn1_post_g,
           mix_pre_g, w_in, q_norm_g, w_uq, kv_norm_g, w_ukv, sink, w_o_a, w_o_b, w_out,
           mix_post_g, ffn2_pre_g, ffn2_w_in, ffn2_w_out, ffn2_post_g):
    assert ffn1_w_in.shape[0] == 1, "single layer"
    bf16 = jnp.bfloat16
    f1 = (ffn1_pre_g, ffn1_w_in[0].astype(bf16), ffn1_w_out[0].astype(bf16), ffn1_post_g)
    f2 = (ffn2_pre_g, ffn2_w_in[0].astype(bf16), ffn2_w_out[0].astype(bf16), ffn2_post_g)
    w_proj, w_gate, wq, wkv = _pack_weights(w_in[0], w_uq[0], w_ukv[0])
    woa, wob, wout = w_o_a[0].astype(bf16), w_o_b[0].astype(bf16), w_out[0].astype(bf16)
    sink = sink[0]

    def front(x, pos):
        tm = _row_tile(x.shape[1])
        x1 = _ffn(x, *f1, tm)
        return x1, _proj(x1, _rope_tables(pos), mix_pre_g, w_proj, q_norm_g, wq,
                         kv_norm_g, wkv, tm)

    lead = jnp.concatenate([jnp.zeros((LEAD_PAD, D_MODEL), x_prompt.dtype),
                            meta_tokens.astype(x_prompt.dtype)], axis=0)[None]
    _, (_, ka_lead, va_lead, _, kw_lead, vw_lead) = front(lead, jnp.arange(BLOCK) - LEAD_PAD)

    def trunk(x):
        s = x.shape[1]
        tm = _row_tile(s)
        x1, (qa, ka, va, qw, kw, vw) = front(x, jnp.arange(s) + N_META)
        o_a = _mla(qa, ka, va, ka_lead, va_lead, tq=256, tk=512)
        o_b = _window(sink, qw, kw, vw, kw_lead, vw_lead, tq=BLOCK)
        x2 = _mixout(x1, o_a, o_b, mix_pre_g, w_gate, woa, wob, wout, mix_post_g, tm)
        return _ffn(x2, *f2, tm)

    return (trunk(x_prompt), trunk(x_sample))
```

```python
import functools

import jax
import jax.numpy as jnp
from jax import lax
from jax.experimental import pallas as pl
from jax.experimental.pallas import tpu as pltpu

D_MODEL = 1024
N_META = 16
BLOCK = 128
WINDOW = 128
LEAD_PAD = BLOCK - N_META
ROPE_THETA = 10000.0
EPS = 1e-6
NEG = -1e30
MLA_HEADS = 8
MLA_Q_LORA = 384
MLA_KV_LORA = 256
MLA_NOPE = 64
MLA_ROPE = 32
MLA_V = 64
SWA_HEADS = 8
SWA_KV_HEADS = 2
SWA_HEAD_DIM = 64
SWA_GROUP = SWA_HEADS // SWA_KV_HEADS
D_FF = 2816

LANES = 128
MLA_SCALE = (MLA_NOPE + MLA_ROPE) ** -0.5
SWA_SCALE = SWA_HEAD_DIM ** -0.5
V7X_VMEM_BYTES = 64 * 1024 * 1024
VMEM_LIMIT = (V7X_VMEM_BYTES * 7) // 8

_C_Q = (0, MLA_Q_LORA)
_C_KV = (_C_Q[1], _C_Q[1] + MLA_KV_LORA)
_C_QS = (_C_KV[1], _C_KV[1] + SWA_HEADS * SWA_HEAD_DIM)
_C_QS_ROT = (_C_QS[1], _C_QS[1] + SWA_HEADS * SWA_HEAD_DIM)
_C_KS = (_C_QS_ROT[1], _C_QS_ROT[1] + LANES)
_C_KS_ROT = (_C_KS[1], _C_KS[1] + LANES)
_C_VS = (_C_KS_ROT[1], _C_KS_ROT[1] + LANES)
_C_KR = (_C_VS[1], _C_VS[1] + LANES)
_C_KR_ROT = (_C_KR[1], _C_KR[1] + LANES)
D_PROJ = _C_KR_ROT[1]

FFN_CHUNK = D_FF // 2


def _const_spec(shape):
    zeros = (0,) * len(shape)
    return pl.BlockSpec(shape, lambda *_: zeros, pipeline_mode=pl.Buffered(1))


def _params(n_axes):
    return pltpu.CompilerParams(
        dimension_semantics=("parallel",) * n_axes, vmem_limit_bytes=VMEM_LIMIT)


def _rms(x, g):
    y = x * lax.rsqrt(jnp.mean(x * x, axis=-1, keepdims=True) + EPS)
    return y * g


def _dot(a, b):
    return jnp.dot(a, b, preferred_element_type=jnp.float32)


def _dot_nt(a, b):
    return lax.dot_general(a, b, (((1,), (1,)), ((), ())),
                           preferred_element_type=jnp.float32)


def _swiglu_residual(x, pre_g, w_in_ref, w_out_ref, post_g):
    h = _rms(x, pre_g).astype(jnp.bfloat16)
    acc = None
    for c in range(D_FF // FFN_CHUNK):
        lo = c * FFN_CHUNK
        g = _dot(h, w_in_ref[:, lo:lo + FFN_CHUNK])
        u = _dot(h, w_in_ref[:, D_FF + lo:D_FF + lo + FFN_CHUNK])
        a = (g * jax.nn.sigmoid(g) * u).astype(jnp.bfloat16)
        part = _dot(a, w_out_ref[lo:lo + FFN_CHUNK, :])
        acc = part if acc is None else acc + part
    return x + 0.5 * _rms(acc, post_g)


def _ffn_kernel(x_ref, pre_g_ref, w_in_ref, w_out_ref, post_g_ref, o_ref):
    o_ref[0] = _swiglu_residual(x_ref[0], pre_g_ref[...], w_in_ref, w_out_ref,
                                post_g_ref[...])


def _ffn(x, pre_g, w_in, w_out, post_g, tm):
    b, s, d = x.shape
    row = pl.BlockSpec((1, tm, d), lambda i, t: (i, t, 0))
    return pl.pallas_call(
        _ffn_kernel,
        grid=(b, s // tm),
        in_specs=[row, _const_spec(pre_g.shape), _const_spec(w_in.shape),
                  _const_spec(w_out.shape), _const_spec(post_g.shape)],
        out_specs=row,
        out_shape=jax.ShapeDtypeStruct(x.shape, jnp.float32),
        compiler_params=_params(2),
        name="ffn",
    )(x, pre_g, w_in, w_out, post_g)


def _proj_kernel(x_ref, tab_ref, pre_g_ref, w_ref, qn_g_ref, wq_ref, kvn_g_ref, wkv_ref,
                 qa_ref, ka_ref, va_ref, qw_ref, kw_ref, vw_ref):
    bf16 = jnp.bfloat16
    h = _rms(x_ref[0], pre_g_ref[...]).astype(bf16)
    tab = tab_ref[...]
    cq_t, sq_t, ck_t, sk_t, cw_t, sw_t = [tab[:, i * LANES:(i + 1) * LANES] for i in range(6)]
    lane = lax.broadcasted_iota(jnp.int32, (1, LANES), 1)
    low_half = lane < SWA_HEAD_DIM

    def cols(rng):
        return _dot(h, w_ref[:, rng[0]:rng[1]])

    qn = _rms(cols(_C_Q), qn_g_ref[...]).astype(bf16)
    qq = _dot(qn, wq_ref[...])
    n_q = MLA_HEADS * LANES
    for hd in range(MLA_HEADS):
        sl = slice(hd * LANES, (hd + 1) * LANES)
        qa_ref[0, hd] = (qq[:, sl] * cq_t + qq[:, n_q + hd * LANES:n_q + (hd + 1) * LANES] * sq_t
                         ).astype(bf16)

    kvn = _rms(cols(_C_KV), kvn_g_ref[...]).astype(bf16)
    kk = _dot(kvn, wkv_ref[...])
    k_rope = cols(_C_KR) * ck_t + cols(_C_KR_ROT) * sk_t
    for hd in range(MLA_HEADS):
        sl = slice(hd * LANES, (hd + 1) * LANES)
        ka_ref[0, hd] = (kk[:, sl] + k_rope).astype(bf16)
        v_slab = kk[:, n_q + hd * LANES:n_q + (hd + 1) * LANES]
        va_ref[0, hd] = jnp.where(lane == MLA_ONES_LANE[hd % 2], 1.0, v_slab).astype(bf16)

    qs = cols(_C_QS)
    qs_rot = cols(_C_QS_ROT)
    for pair in range(SWA_HEADS // 2):
        sl = slice(pair * LANES, (pair + 1) * LANES)
        q_pair = (qs[:, sl] * cw_t + qs_rot[:, sl] * sw_t) * SWA_SCALE
        qw_ref[0, 2 * pair] = jnp.where(low_half, q_pair, 0.0).astype(bf16)
        qw_ref[0, 2 * pair + 1] = jnp.where(low_half, 0.0, q_pair).astype(bf16)

    k_pair = cols(_C_KS) * cw_t + cols(_C_KS_ROT) * sw_t
    v_pair = cols(_C_VS)
    for src, dst in ((k_pair, kw_ref), (v_pair, vw_ref)):
        swapped = pltpu.roll(src, SWA_HEAD_DIM, 1)
        dst[0, 0] = jnp.where(low_half, src, swapped).astype(bf16)
        dst[0, 1] = jnp.where(low_half, swapped, src).astype(bf16)


def _proj(x1, tab, pre_g, w_proj, qn_g, wq, kvn_g, wkv, tm):
    b, s, d = x1.shape
    row = pl.BlockSpec((1, tm, d), lambda i, t: (i, t, 0))
    tab_spec = pl.BlockSpec((tm, tab.shape[1]), lambda i, t: (t, 0))

    def heads(n):
        return pl.BlockSpec((1, n, tm, LANES), lambda i, t: (i, 0, t, 0))

    def out(n):
        return jax.ShapeDtypeStruct((b, n, s, LANES), jnp.bfloat16)

    return pl.pallas_call(
        _proj_kernel,
        grid=(b, s // tm),
        in_specs=[row, tab_spec, _const_spec(pre_g.shape), _const_spec(w_proj.shape),
                  _const_spec(qn_g.shape), _const_spec(wq.shape),
                  _const_spec(kvn_g.shape), _const_spec(wkv.shape)],
        out_specs=[heads(MLA_HEADS), heads(MLA_HEADS), heads(MLA_HEADS),
                   heads(SWA_HEADS), heads(SWA_KV_HEADS), heads(SWA_KV_HEADS)],
        out_shape=[out(MLA_HEADS), out(MLA_HEADS), out(MLA_HEADS),
                   out(SWA_HEADS), out(SWA_KV_HEADS), out(SWA_KV_HEADS)],
        compiler_params=_params(2),
        name="proj",
    )(x1, tab, pre_g, w_proj, qn_g, wq, kvn_g, wkv)


MLA_HEADS_PER_STEP = 2
MLA_ONES_LANE = (MLA_V, 0)
LOG2_E = 1.4426950408889634


def _mla_kernel(q_ref, k_ref, v_ref, kl_ref, vl_ref, o_ref, s_a, s_b, *, tq, ck):
    seq = k_ref.shape[2]
    n_chunks = seq // ck
    n_tiles = seq // tq
    lane = lax.broadcasted_iota(jnp.int32, (1, LANES), 1)
    lead_ok = lane >= LEAD_PAD

    for j in range(MLA_HEADS_PER_STEP):
        ones = MLA_ONES_LANE[j % 2]

        def rows(t):
            return pl.ds(pl.multiple_of(t * tq, tq), tq)

        def keys(c):
            return pl.ds(pl.multiple_of(c * ck, ck), ck)

        def score_chunk(q, c, s_new, mx, j=j):
            s = _dot_nt(q, k_ref[0, j, keys(c), :])
            s_new[c] = s
            for i in range(ck // LANES):
                mx = jnp.maximum(mx, s[:, i * LANES:(i + 1) * LANES])
            return mx

        def value_chunk(c, s_old, m, acc, j=j):
            p = jnp.exp2(s_old[c] - m).astype(jnp.bfloat16)
            return acc + _dot(p, v_ref[0, j, keys(c), :])

        def lead_scores(q, j=j):
            return jnp.where(lead_ok, _dot_nt(q, kl_ref[0, j]), NEG)

        def row_max(mx, s_lead, j=j):
            m = jnp.max(mx, axis=-1, keepdims=True)
            return m, _dot(jnp.exp2(s_lead - m).astype(jnp.bfloat16), vl_ref[0, j])

        def emit(t, acc, j=j, ones=ones):
            o = (acc * (1.0 / acc[:, ones:ones + 1])).astype(o_ref.dtype)
            if j == 0:
                o_ref[0, rows(t), :] = o
            else:
                o_ref[0, rows(t), :] = jnp.where(lane < MLA_V, o_ref[0, rows(t), :], o)

        def stage(t, s_new, s_old, m_prev, acc_prev, j=j):
            q = q_ref[0, j, rows(t), :]
            s_lead = lead_scores(q)

            def body(c, carry):
                mx, acc = carry
                acc = value_chunk(c, s_old, m_prev, acc)
                mx = score_chunk(q, c, s_new, mx)
                return mx, acc

            mx, acc = lax.fori_loop(0, n_chunks, body, (s_lead, acc_prev))
            emit(t - 1, acc)
            return row_max(mx, s_lead)

        q0 = q_ref[0, j, rows(0), :]
        s_lead0 = lead_scores(q0)
        mx0 = lax.fori_loop(0, n_chunks, lambda c, mx, q0=q0: score_chunk(q0, c, s_a, mx),
                            s_lead0)
        carry = row_max(mx0, s_lead0)

        def pair(i, carry):
            m, acc = stage(2 * i + 1, s_b, s_a, *carry)
            return stage(2 * i + 2, s_a, s_b, m, acc)

        carry = lax.fori_loop(0, n_tiles // 2 - 1, pair, carry)
        m, acc = stage(n_tiles - 1, s_b, s_a, *carry)
        acc = lax.fori_loop(0, n_chunks, lambda c, acc, m=m: value_chunk(c, s_b, m, acc), acc)
        emit(n_tiles - 1, acc)


def _mla(q, k, v, k_lead, v_lead, tq, ck):
    b, nh, s, _ = q.shape
    hp = MLA_HEADS_PER_STEP
    assert s % ck == 0 and s % (2 * tq) == 0
    seq_spec = pl.BlockSpec((1, hp, s, LANES), lambda i, p: (i, p, 0, 0))
    lead_spec = pl.BlockSpec((1, hp, BLOCK, LANES), lambda i, p: (0, p, 0, 0))
    score_buf = pltpu.VMEM((s // ck, tq, ck), jnp.float32)
    return pl.pallas_call(
        functools.partial(_mla_kernel, tq=tq, ck=ck),
        grid=(b, nh // hp),
        in_specs=[seq_spec, seq_spec, seq_spec, lead_spec, lead_spec],
        out_specs=pl.BlockSpec((1, s, LANES), lambda i, p: (i, 0, p)),
        out_shape=jax.ShapeDtypeStruct((b, s, nh * MLA_V), jnp.bfloat16),
        scratch_shapes=[score_buf, score_buf],
        compiler_params=_params(2),
        name="mla",
    )(q, k, v, k_lead, v_lead)


def _window_kernel(sink_ref, q_ref, kl_ref, kc_ref, kr_ref, km_ref,
                   vl_ref, vc_ref, vr_ref, vm_ref, o_ref):
    tq = q_ref.shape[2]
    t = pl.program_id(1)
    n_t = pl.num_programs(1)
    row = lax.broadcasted_iota(jnp.int32, (tq, BLOCK), 0)
    col = lax.broadcasted_iota(jnp.int32, (tq, BLOCK), 1)
    ok_left = jnp.logical_and(row - (col - BLOCK) <= WINDOW, t > 0)
    ok_right = jnp.logical_and((tq + col) - row <= WINDOW, t < n_t - 1)
    rowc = lax.broadcasted_iota(jnp.int32, (tq, tq), 0)
    colc = lax.broadcasted_iota(jnp.int32, (tq, tq), 1)
    ok_center = jnp.abs(rowc - colc) <= WINDOW
    ok_meta = col >= LEAD_PAD
    low_half = lax.broadcasted_iota(jnp.int32, (1, LANES), 1) < SWA_HEAD_DIM

    for g in range(SWA_KV_HEADS):
        q = q_ref[0, g * SWA_GROUP:(g + 1) * SWA_GROUP].reshape(SWA_GROUP * tq, LANES)
        pieces = []
        for k_ref, v_ref, ok in ((kl_ref, vl_ref, ok_left), (kc_ref, vc_ref, ok_center),
                                 (kr_ref, vr_ref, ok_right), (km_ref, vm_ref, ok_meta)):
            s = _dot_nt(q, k_ref[0, g]).reshape(SWA_GROUP, tq, -1)
            pieces.append((jnp.where(ok[None], s, NEG), v_ref))
        sink = jnp.concatenate(
            [jnp.full((1, tq, 1), sink_ref[g * SWA_GROUP + r], jnp.float32)
             for r in range(SWA_GROUP)], axis=0)
        m = sink
        for s, _ in pieces:
            m = jnp.maximum(m, jnp.max(s, axis=-1, keepdims=True))
        l = jnp.exp(sink - m)
        acc = None
        for s, v_ref in pieces:
            p = jnp.exp(s - m)
            l = l + jnp.sum(p, axis=-1, keepdims=True)
            pv = _dot(p.astype(jnp.bfloat16).reshape(SWA_GROUP * tq, -1), v_ref[0, g])
            acc = pv if acc is None else acc + pv
        o = acc.reshape(SWA_GROUP, tq, LANES) * (1.0 / l)
        for pair in range(SWA_GROUP // 2):
            slab = g * (SWA_GROUP // 2) + pair
            o_ref[0, :, slab * LANES:(slab + 1) * LANES] = jnp.where(
                low_half, o[2 * pair], o[2 * pair + 1]).astype(o_ref.dtype)


def _window(sink, q, k, v, k_lead, v_lead, tq):
    b, nh, s, _ = q.shape
    n_blk = s // BLOCK
    per = tq // BLOCK
    q_spec = pl.BlockSpec((1, nh, tq, LANES), lambda i, t: (i, 0, t, 0))
    left = pl.BlockSpec((1, SWA_KV_HEADS, BLOCK, LANES),
                        lambda i, t: (i, 0, jnp.maximum(t * per - 1, 0), 0))
    center = pl.BlockSpec((1, SWA_KV_HEADS, tq, LANES), lambda i, t: (i, 0, t, 0))
    right = pl.BlockSpec((1, SWA_KV_HEADS, BLOCK, LANES),
                         lambda i, t: (i, 0, jnp.minimum((t + 1) * per, n_blk - 1), 0))
    lead = pl.BlockSpec((1, SWA_KV_HEADS, BLOCK, LANES), lambda i, t: (0, 0, 0, 0))
    smem = pl.BlockSpec(memory_space=pltpu.SMEM)
    return pl.pallas_call(
        _window_kernel,
        grid=(b, s // tq),
        in_specs=[smem, q_spec, left, center, right, lead, left, center, right, lead],
        out_specs=pl.BlockSpec((1, tq, nh * SWA_HEAD_DIM), lambda i, t: (i, t, 0)),
        out_shape=jax.ShapeDtypeStruct((b, s, nh * SWA_HEAD_DIM), jnp.bfloat16),
        compiler_params=_params(2),
        name="window",
    )(sink, q, k, k, k, k_lead, v, v, v, v_lead)


def _mixout_kernel(x_ref, oa_ref, ob_ref, pre_g_ref, wg_ref, woa_ref, wob_ref, wout_ref,
                   post_g_ref, o_ref):
    x1 = x_ref[0]
    h = _rms(x1, pre_g_ref[...]).astype(jnp.bfloat16)
    y_a = _dot(oa_ref[0], woa_ref[...])
    y_b = _dot(ob_ref[0], wob_ref[...])
    merged = (jax.nn.sigmoid(_dot(h, wg_ref[:, :D_MODEL])) * y_a
              + jax.nn.sigmoid(_dot(h, wg_ref[:, D_MODEL:])) * y_b)
    o_ref[0] = x1 + _rms(_dot(merged.astype(jnp.bfloat16), wout_ref[...]), post_g_ref[...])


def _mixout(x1, o_a, o_b, pre_g, wg, woa, wob, wout, post_g, tm):
    b, s, d = x1.shape
    row = pl.BlockSpec((1, tm, d), lambda i, t: (i, t, 0))
    half = pl.BlockSpec((1, tm, o_a.shape[2]), lambda i, t: (i, t, 0))
    return pl.pallas_call(
        _mixout_kernel,
        grid=(b, s // tm),
        in_specs=[row, half, half, _const_spec(pre_g.shape), _const_spec(wg.shape),
                  _const_spec(woa.shape), _const_spec(wob.shape), _const_spec(wout.shape),
                  _const_spec(post_g.shape)],
        out_specs=row,
        out_shape=jax.ShapeDtypeStruct(x1.shape, jnp.float32),
        compiler_params=_params(2),
        name="mixout",
    )(x1, o_a, o_b, pre_g, wg, woa, wob, wout, post_g)


def _rot_cols(w, n_heads, dim):
    w = w.reshape(w.shape[0], n_heads, dim)
    half = dim // 2
    return jnp.concatenate([-w[..., half:], w[..., :half]], axis=-1).reshape(w.shape[0], -1)


def _pack_weights(w_in, w_uq, w_ukv):
    bf16 = jnp.bfloat16
    d = w_in.shape[0]
    o = 0
    parts = []
    for n in (MLA_Q_LORA, MLA_KV_LORA, MLA_ROPE, SWA_HEADS * SWA_HEAD_DIM,
              SWA_KV_HEADS * SWA_HEAD_DIM, SWA_KV_HEADS * SWA_HEAD_DIM, D_MODEL, D_MODEL):
        parts.append(w_in[:, o:o + n])
        o += n
    w_cq, w_ckv, w_kr, w_qs, w_ks, w_vs, w_ga, w_gb = parts

    def rope_slab(w):
        return jnp.concatenate([jnp.zeros((d, MLA_NOPE), w.dtype), w,
                                jnp.zeros((d, LANES - MLA_NOPE - MLA_ROPE), w.dtype)], axis=1)

    w_proj = jnp.concatenate(
        [w_cq, w_ckv, w_qs, _rot_cols(w_qs, SWA_HEADS, SWA_HEAD_DIM),
         w_ks, _rot_cols(w_ks, SWA_KV_HEADS, SWA_HEAD_DIM), w_vs,
         rope_slab(w_kr), rope_slab(_rot_cols(w_kr, 1, MLA_ROPE))], axis=1).astype(bf16)
    w_gate = jnp.concatenate([w_ga, w_gb], axis=1).astype(bf16)

    r = w_uq.shape[0]
    uq = w_uq.reshape(r, MLA_HEADS, MLA_NOPE + MLA_ROPE)
    pad = jnp.zeros((r, MLA_HEADS, LANES - MLA_NOPE - MLA_ROPE), w_uq.dtype)
    q_plain = jnp.concatenate([uq, pad], axis=-1)
    q_rot = jnp.concatenate(
        [jnp.zeros((r, MLA_HEADS, MLA_NOPE), w_uq.dtype),
         _rot_cols(uq[..., MLA_NOPE:].reshape(r, -1), MLA_HEADS, MLA_ROPE).reshape(
             r, MLA_HEADS, MLA_ROPE), pad], axis=-1)
    wq = jnp.concatenate([q_plain.reshape(r, -1), q_rot.reshape(r, -1)], axis=1).astype(bf16)

    r = w_ukv.shape[0]
    ukv = w_ukv.reshape(r, MLA_HEADS, MLA_NOPE + MLA_V)
    zeros = jnp.zeros((r, MLA_HEADS, LANES - MLA_NOPE), w_ukv.dtype)
    k_slab = jnp.concatenate([ukv[..., :MLA_NOPE], zeros], axis=-1)
    v_h = ukv[..., MLA_NOPE:]
    zv = jnp.zeros_like(v_h)
    odd = (jnp.arange(MLA_HEADS) % 2 == 1)[None, :, None]
    v_slab = jnp.concatenate([jnp.where(odd, zv, v_h), jnp.where(odd, v_h, zv)], axis=-1)
    wkv = jnp.concatenate([k_slab.reshape(r, -1), v_slab.reshape(r, -1)], axis=1).astype(bf16)
    return w_proj, w_gate, wq, wkv


def _rope_tables(pos):
    pos = pos.astype(jnp.float32)[:, None]

    def cs(half):
        inv = ROPE_THETA ** (-jnp.arange(half, dtype=jnp.float32) / half)
        ang = pos * inv[None, :]
        return jnp.cos(ang), jnp.sin(ang)

    n = pos.shape[0]
    c16, s16 = cs(MLA_ROPE // 2)
    c32, s32 = cs(SWA_HEAD_DIM // 2)
    ones = jnp.ones((n, MLA_NOPE), jnp.float32)
    zeros = jnp.zeros((n, MLA_NOPE), jnp.float32)
    tail = jnp.zeros((n, LANES - MLA_NOPE - MLA_ROPE), jnp.float32)
    ck = jnp.concatenate([ones, c16, c16, tail], axis=1)
    sk = jnp.concatenate([zeros, s16, s16, tail], axis=1)
    cw = jnp.concatenate([c32] * 4, axis=1)
    sw = jnp.concatenate([s32] * 4, axis=1)
    q_scale = MLA_SCALE * LOG2_E
    return jnp.concatenate([ck * q_scale, sk * q_scale, ck, sk, cw, sw], axis=1)


def _row_tile(s):
    return min(s, 512)


def kernel(x_prompt, x_sample, meta_tokens, ffn1_pre_g, ffn1_w_in, ffn1_w_out, ffn1_post_g,
           mix_pre_g, w_in, q_norm_g, w_uq, kv_norm_g, w_ukv, sink, w_o_a, w_o_b, w_out,
           mix_post_g, ffn2_pre_g, ffn2_w_in, ffn2_w_out, ffn2_post_g):
    assert ffn1_w_in.shape[0] == 1, "single layer"
    bf16 = jnp.bfloat16
    f1 = (ffn1_pre_g, ffn1_w_in[0].astype(bf16), ffn1_w_out[0].astype(bf16), ffn1_post_g)
    f2 = (ffn2_pre_g, ffn2_w_in[0].astype(bf16), ffn2_w_out[0].astype(bf16), ffn2_post_g)
    w_proj, w_gate, wq, wkv = _pack_weights(w_in[0], w_uq[0], w_ukv[0])
    woa, wob, wout = w_o_a[0].astype(bf16), w_o_b[0].astype(bf16), w_out[0].astype(bf16)
    sink = sink[0]

    def front(x, pos):
        tm = _row_tile(x.shape[1])
        x1 = _ffn(x, *f1, tm)
        return x1, _proj(x1, _rope_tables(pos), mix_pre_g, w_proj, q_norm_g, wq,
                         kv_norm_g, wkv, tm)

    lead = jnp.concatenate([jnp.zeros((LEAD_PAD, D_MODEL), x_prompt.dtype),
                            meta_tokens.astype(x_prompt.dtype)], axis=0)[None]
    _, (_, ka_lead, va_lead, _, kw_lead, vw_lead) = front(lead, jnp.arange(BLOCK) - LEAD_PAD)

    def trunk(x):
        s = x.shape[1]
        tm = _row_tile(s)
        x1, (qa, ka, va, qw, kw, vw) = front(x, jnp.arange(s) + N_META)
        o_a = _mla(qa, ka, va, ka_lead, va_lead, tq=256, ck=2048)
        o_b = _window(sink, qw, kw, vw, kw_lead, vw_lead, tq=BLOCK)
        x2 = _mixout(x1, o_a, o_b, mix_pre_g, w_gate, woa, wob, wout, mix_post_g, tm)
        return _ffn(x2, *f2, tm)

    return (trunk(x_prompt), trunk(x_sample))
```

```python
import functools

import jax
import jax.numpy as jnp
from jax import lax
from jax.experimental import pallas as pl
from jax.experimental.pallas import tpu as pltpu

D_MODEL = 1024
N_META = 16
BLOCK = 128
WINDOW = 128
LEAD_PAD = BLOCK - N_META
ROPE_THETA = 10000.0
EPS = 1e-6
NEG = -1e30
MLA_HEADS = 8
MLA_Q_LORA = 384
MLA_KV_LORA = 256
MLA_NOPE = 64
MLA_ROPE = 32
MLA_V = 64
SWA_HEADS = 8
SWA_KV_HEADS = 2
SWA_HEAD_DIM = 64
SWA_GROUP = SWA_HEADS // SWA_KV_HEADS
D_FF = 2816

LANES = 128
MLA_SCALE = (MLA_NOPE + MLA_ROPE) ** -0.5
SWA_SCALE = SWA_HEAD_DIM ** -0.5
V7X_VMEM_BYTES = 64 * 1024 * 1024
VMEM_LIMIT = (V7X_VMEM_BYTES * 7) // 8

_C_Q = (0, MLA_Q_LORA)
_C_KV = (_C_Q[1], _C_Q[1] + MLA_KV_LORA)
_C_QS = (_C_KV[1], _C_KV[1] + SWA_HEADS * SWA_HEAD_DIM)
_C_QS_ROT = (_C_QS[1], _C_QS[1] + SWA_HEADS * SWA_HEAD_DIM)
_C_KS = (_C_QS_ROT[1], _C_QS_ROT[1] + LANES)
_C_KS_ROT = (_C_KS[1], _C_KS[1] + LANES)
_C_VS = (_C_KS_ROT[1], _C_KS_ROT[1] + LANES)
_C_KR = (_C_VS[1], _C_VS[1] + LANES)
_C_KR_ROT = (_C_KR[1], _C_KR[1] + LANES)
D_PROJ = _C_KR_ROT[1]

FFN_CHUNK = D_FF // 2


def _const_spec(shape):
    zeros = (0,) * len(shape)
    return pl.BlockSpec(shape, lambda *_: zeros, pipeline_mode=pl.Buffered(1))


def _params(n_axes):
    return pltpu.CompilerParams(
        dimension_semantics=("parallel",) * n_axes, vmem_limit_bytes=VMEM_LIMIT)


def _rms(x, g):
    y = x * lax.rsqrt(jnp.mean(x * x, axis=-1, keepdims=True) + EPS)
    return y * g


def _dot(a, b):
    return jnp.dot(a, b, preferred_element_type=jnp.float32)


def _dot_nt(a, b):
    return lax.dot_general(a, b, (((1,), (1,)), ((), ())),
                           preferred_element_type=jnp.float32)


def _swiglu_residual(x, pre_g, w_in_ref, w_out_ref, post_g):
    h = _rms(x, pre_g).astype(jnp.bfloat16)
    acc = None
    for c in range(D_FF // FFN_CHUNK):
        lo = c * FFN_CHUNK
        g = _dot(h, w_in_ref[:, lo:lo + FFN_CHUNK])
        u = _dot(h, w_in_ref[:, D_FF + lo:D_FF + lo + FFN_CHUNK])
        a = (g * jax.nn.sigmoid(g) * u).astype(jnp.bfloat16)
        part = _dot(a, w_out_ref[lo:lo + FFN_CHUNK, :])
        acc = part if acc is None else acc + part
    return x + 0.5 * _rms(acc, post_g)


def _ffn_kernel(x_ref, pre_g_ref, w_in_ref, w_out_ref, post_g_ref, o_ref):
    o_ref[0] = _swiglu_residual(x_ref[0], pre_g_ref[...], w_in_ref, w_out_ref,
                                post_g_ref[...])


def _ffn(x, pre_g, w_in, w_out, post_g, tm):
    b, s, d = x.shape
    row = pl.BlockSpec((1, tm, d), lambda i, t: (i, t, 0))
    return pl.pallas_call(
        _ffn_kernel,
        grid=(b, s // tm),
        in_specs=[row, _const_spec(pre_g.shape), _const_spec(w_in.shape),
                  _const_spec(w_out.shape), _const_spec(post_g.shape)],
        out_specs=row,
        out_shape=jax.ShapeDtypeStruct(x.shape, jnp.float32),
        compiler_params=_params(2),
        name="ffn",
    )(x, pre_g, w_in, w_out, post_g)


def _proj_kernel(x_ref, ck_ref, sk_ref, cw_ref, sw_ref, pre_g_ref, w_ref, qn_g_ref, wq_ref,
                 kvn_g_ref, wkv_ref, qa_ref, ka_ref, va_ref, qw_ref, kw_ref, vw_ref):
    bf16 = jnp.bfloat16
    h = _rms(x_ref[0], pre_g_ref[...]).astype(bf16)
    ck_t, sk_t, cw_t, sw_t = ck_ref[...], sk_ref[...], cw_ref[...], sw_ref[...]
    lane = lax.broadcasted_iota(jnp.int32, (1, LANES), 1)
    low_half = lane < SWA_HEAD_DIM

    def cols(rng):
        return _dot(h, w_ref[:, rng[0]:rng[1]])

    qn = _rms(cols(_C_Q), qn_g_ref[...]).astype(bf16)
    qq = _dot(qn, wq_ref[...])
    n_q = MLA_HEADS * LANES
    for hd in range(MLA_HEADS):
        sl = slice(hd * LANES, (hd + 1) * LANES)
        q_rope = qq[:, sl] * ck_t + qq[:, n_q + hd * LANES:n_q + (hd + 1) * LANES] * sk_t
        qa_ref[0, hd] = (q_rope * (MLA_SCALE * LOG2_E)).astype(bf16)

    kvn = _rms(cols(_C_KV), kvn_g_ref[...]).astype(bf16)
    kk = _dot(kvn, wkv_ref[...])
    k_rope = cols(_C_KR) * ck_t + cols(_C_KR_ROT) * sk_t
    for hd in range(MLA_HEADS):
        sl = slice(hd * LANES, (hd + 1) * LANES)
        ka_ref[0, hd] = (kk[:, sl] + k_rope).astype(bf16)
        v_slab = kk[:, n_q + hd * LANES:n_q + (hd + 1) * LANES]
        va_ref[0, hd] = jnp.where(lane == MLA_ONES_LANE[hd % 2], 1.0, v_slab).astype(bf16)

    qs = cols(_C_QS)
    qs_rot = cols(_C_QS_ROT)
    for pair in range(SWA_HEADS // 2):
        sl = slice(pair * LANES, (pair + 1) * LANES)
        q_pair = (qs[:, sl] * cw_t + qs_rot[:, sl] * sw_t) * (SWA_SCALE * LOG2_E)
        qw_ref[0, 2 * pair] = jnp.where(low_half, q_pair, 0.0).astype(bf16)
        qw_ref[0, 2 * pair + 1] = jnp.where(low_half, 0.0, q_pair).astype(bf16)

    k_pair = cols(_C_KS) * cw_t + cols(_C_KS_ROT) * sw_t
    k_swapped = pltpu.roll(k_pair, SWA_HEAD_DIM, 1)
    kw_ref[0, 0] = jnp.where(low_half, k_pair, k_swapped).astype(bf16)
    kw_ref[0, 1] = jnp.where(low_half, k_swapped, k_pair).astype(bf16)

    v_pair = cols(_C_VS)
    v_swapped = pltpu.roll(v_pair, SWA_HEAD_DIM, 1)
    vw_ref[0, 0, :, :LANES] = jnp.where(low_half, v_pair, 1.0).astype(bf16)
    vw_ref[0, 0, :, LANES:] = jnp.where(low_half, 1.0, v_swapped).astype(bf16)
    vw_ref[0, 1, :, :LANES] = jnp.where(low_half, v_swapped, 1.0).astype(bf16)
    vw_ref[0, 1, :, LANES:] = jnp.where(low_half, 1.0, v_pair).astype(bf16)


def _proj(x1, tabs, pre_g, w_proj, qn_g, wq, kvn_g, wkv, tm):
    b, s, d = x1.shape
    row = pl.BlockSpec((1, tm, d), lambda i, t: (i, t, 0))
    tab_spec = pl.BlockSpec((tm, LANES), lambda i, t: (t, 0))

    def heads(n, lanes):
        return pl.BlockSpec((1, n, tm, lanes), lambda i, t: (i, 0, t, 0))

    def out(n, lanes):
        return jax.ShapeDtypeStruct((b, n, s, lanes), jnp.bfloat16)

    outs = ((MLA_HEADS, LANES), (MLA_HEADS, LANES), (MLA_HEADS, LANES),
            (SWA_HEADS, LANES), (SWA_KV_HEADS, LANES), (SWA_KV_HEADS, WIN_V_LANES))
    return pl.pallas_call(
        _proj_kernel,
        grid=(b, s // tm),
        in_specs=[row] + [tab_spec] * len(tabs) + [
            _const_spec(pre_g.shape), _const_spec(w_proj.shape), _const_spec(qn_g.shape),
            _const_spec(wq.shape), _const_spec(kvn_g.shape), _const_spec(wkv.shape)],
        out_specs=[heads(*o) for o in outs],
        out_shape=[out(*o) for o in outs],
        compiler_params=_params(2),
        name="proj",
    )(x1, *tabs, pre_g, w_proj, qn_g, wq, kvn_g, wkv)


MLA_HEADS_PER_STEP = 2
MLA_ONES_LANE = (MLA_V, 0)
LOG2_E = 1.4426950408889634


def _mla_kernel(q_ref, k_ref, v_ref, kl_ref, vl_ref, o_ref, s_a, s_b, *, tq, ck):
    seq = k_ref.shape[2]
    n_chunks = seq // ck
    n_tiles = seq // tq
    lane = lax.broadcasted_iota(jnp.int32, (1, LANES), 1)
    lead_ok = lane >= LEAD_PAD

    for j in range(MLA_HEADS_PER_STEP):
        ones = MLA_ONES_LANE[j % 2]

        def rows(t):
            return pl.ds(pl.multiple_of(t * tq, tq), tq)

        def keys(c):
            return pl.ds(c * ck, ck)

        def score_chunk(q, c, s_new, mx, j=j):
            s = _dot_nt(q, k_ref[0, j, keys(c), :])
            s_new[c] = s
            for i in range(ck // LANES):
                mx = jnp.maximum(mx, s[:, i * LANES:(i + 1) * LANES])
            return mx

        def value_chunk(c, s_old, m, acc, j=j):
            p = jnp.exp2(s_old[c] - m).astype(jnp.bfloat16)
            return acc + _dot(p, v_ref[0, j, keys(c), :])

        def lead_scores(q, j=j):
            return jnp.where(lead_ok, _dot_nt(q, kl_ref[0, j]), NEG)

        def row_max(mx, s_lead, j=j):
            m = jnp.max(mx, axis=-1, keepdims=True)
            return m, _dot(jnp.exp2(s_lead - m).astype(jnp.bfloat16), vl_ref[0, j])

        def emit(t, acc, j=j, ones=ones):
            o = (acc * (1.0 / acc[:, ones:ones + 1])).astype(o_ref.dtype)
            if j == 0:
                o_ref[0, rows(t), :] = o
            else:
                o_ref[0, rows(t), :] = jnp.where(lane < MLA_V, o_ref[0, rows(t), :], o)

        def stage(t, s_new, s_old, m_prev, acc_prev, j=j):
            q = q_ref[0, j, rows(t), :]
            s_lead = lead_scores(q)
            mx, acc = s_lead, acc_prev
            for c in range(n_chunks):
                mx = score_chunk(q, c, s_new, mx)
                acc = value_chunk(c, s_old, m_prev, acc)
            emit(t - 1, acc)
            return row_max(mx, s_lead)

        q0 = q_ref[0, j, rows(0), :]
        s_lead0 = lead_scores(q0)
        mx0 = s_lead0
        for c in range(n_chunks):
            mx0 = score_chunk(q0, c, s_a, mx0)
        carry = row_max(mx0, s_lead0)

        def pair(i, carry):
            m, acc = stage(2 * i + 1, s_b, s_a, *carry)
            return stage(2 * i + 2, s_a, s_b, m, acc)

        carry = lax.fori_loop(0, n_tiles // 2 - 1, pair, carry)
        m, acc = stage(n_tiles - 1, s_b, s_a, *carry)
        for c in range(n_chunks):
            acc = value_chunk(c, s_b, m, acc)
        emit(n_tiles - 1, acc)


def _mla(q, k, v, k_lead, v_lead, tq, ck):
    b, nh, s, _ = q.shape
    hp = MLA_HEADS_PER_STEP
    assert s % ck == 0 and s % (2 * tq) == 0
    seq_spec = pl.BlockSpec((1, hp, s, LANES), lambda i, p: (i, p, 0, 0))
    lead_spec = pl.BlockSpec((1, hp, BLOCK, LANES), lambda i, p: (0, p, 0, 0))
    score_buf = pltpu.VMEM((s // ck, tq, ck), jnp.float32)
    return pl.pallas_call(
        functools.partial(_mla_kernel, tq=tq, ck=ck),
        grid=(b, nh // hp),
        in_specs=[seq_spec, seq_spec, seq_spec, lead_spec, lead_spec],
        out_specs=pl.BlockSpec((1, s, LANES), lambda i, p: (i, 0, p)),
        out_shape=jax.ShapeDtypeStruct((b, s, nh * MLA_V), jnp.bfloat16),
        scratch_shapes=[score_buf, score_buf],
        compiler_params=_params(2),
        name="mla",
    )(q, k, v, k_lead, v_lead)


WIN_V_LANES = 2 * LANES


def _window_kernel(sink_ref, q_ref, kl_ref, kc_ref, kr_ref, km_ref,
                   vl_ref, vc_ref, vr_ref, vm_ref, o_ref):
    tq = q_ref.shape[2]
    n_blk = tq // BLOCK
    t = pl.program_id(1)
    n_t = pl.num_programs(1)
    n_keys = 4 * BLOCK
    row = lax.broadcasted_iota(jnp.int32, (BLOCK, n_keys), 0)
    col = lax.broadcasted_iota(jnp.int32, (BLOCK, n_keys), 1)
    in_left = jnp.logical_and(col < BLOCK, col >= row)
    in_right = jnp.logical_and(jnp.logical_and(col >= 2 * BLOCK, col < 3 * BLOCK),
                               col - 2 * BLOCK <= row)
    in_rest = jnp.logical_or(jnp.logical_and(col >= BLOCK, col < 2 * BLOCK),
                             col >= 3 * BLOCK + LEAD_PAD)
    low_half = lax.broadcasted_iota(jnp.int32, (1, LANES), 1) < SWA_HEAD_DIM

    def keys_of(blk, left_ref, center_ref, right_ref, lead_ref, g):
        first, last = blk == 0, blk == n_blk - 1
        return jnp.concatenate(
            [left_ref[0, g] if first else center_ref[0, g, (blk - 1) * BLOCK:blk * BLOCK, :],
             center_ref[0, g, blk * BLOCK:(blk + 1) * BLOCK, :],
             right_ref[0, g] if last else center_ref[0, g, (blk + 1) * BLOCK:(blk + 2) * BLOCK, :],
             lead_ref[0, g]], axis=0)

    def scores(g, blk):
        ok_left = jnp.logical_and(in_left, t > 0) if blk == 0 else in_left
        ok_right = jnp.logical_and(in_right, t < n_t - 1) if blk == n_blk - 1 else in_right
        ok = jnp.logical_or(jnp.logical_or(ok_left, ok_right), in_rest)
        q = q_ref[0, g * SWA_GROUP:(g + 1) * SWA_GROUP, blk * BLOCK:(blk + 1) * BLOCK, :]
        s = _dot_nt(q.reshape(SWA_GROUP * BLOCK, LANES),
                    keys_of(blk, kl_ref, kc_ref, kr_ref, km_ref, g))
        return jnp.where(ok[None], s.reshape(SWA_GROUP, BLOCK, n_keys), NEG)

    def finish(g, blk, s):
        sink = jnp.concatenate(
            [jnp.full((1, BLOCK, 1), sink_ref[g * SWA_GROUP + r] * LOG2_E, jnp.float32)
             for r in range(SWA_GROUP)], axis=0)
        m = jnp.maximum(jnp.max(s, axis=-1, keepdims=True), sink)
        p = jnp.exp2(s - m).astype(jnp.bfloat16).reshape(SWA_GROUP * BLOCK, n_keys)
        tail = jnp.exp2(sink - m)
        acc = _dot(p, keys_of(blk, vl_ref, vc_ref, vr_ref, vm_ref, g)).reshape(
            SWA_GROUP, BLOCK, WIN_V_LANES)
        for pair in range(SWA_GROUP // 2):
            even, odd = acc[2 * pair], acc[2 * pair + 1]
            num = jnp.where(low_half, even[:, :LANES], odd[:, LANES:])
            den = (jnp.where(low_half, even[:, LANES:], odd[:, :LANES])
                   + jnp.where(low_half, tail[2 * pair], tail[2 * pair + 1]))
            slab = g * (SWA_GROUP // 2) + pair
            o_ref[0, blk * BLOCK:(blk + 1) * BLOCK, slab * LANES:(slab + 1) * LANES] = (
                num * (1.0 / den)).astype(o_ref.dtype)

    units = [(g, blk) for g in range(SWA_KV_HEADS) for blk in range(n_blk)]
    s_next = scores(*units[0])
    for i, unit in enumerate(units):
        s_cur = s_next
        if i + 1 < len(units):
            s_next = scores(*units[i + 1])
        finish(*unit, s_cur)


def _window(sink, q, k, v, k_lead, v_lead, tq):
    b, nh, s, _ = q.shape
    n_blk = s // BLOCK
    per = tq // BLOCK

    def specs(n, lanes):
        left = pl.BlockSpec((1, n, BLOCK, lanes),
                            lambda i, t: (i, 0, jnp.maximum(t * per - 1, 0), 0))
        center = pl.BlockSpec((1, n, tq, lanes), lambda i, t: (i, 0, t, 0))
        right = pl.BlockSpec((1, n, BLOCK, lanes),
                             lambda i, t: (i, 0, jnp.minimum((t + 1) * per, n_blk - 1), 0))
        lead = pl.BlockSpec((1, n, BLOCK, lanes), lambda i, t: (0, 0, 0, 0))
        return [left, center, right, lead]

    smem = pl.BlockSpec(memory_space=pltpu.SMEM)
    return pl.pallas_call(
        _window_kernel,
        grid=(b, s // tq),
        in_specs=([smem, specs(nh, LANES)[1]] + specs(SWA_KV_HEADS, LANES)
                  + specs(SWA_KV_HEADS, WIN_V_LANES)),
        out_specs=pl.BlockSpec((1, tq, nh * SWA_HEAD_DIM), lambda i, t: (i, t, 0)),
        out_shape=jax.ShapeDtypeStruct((b, s, nh * SWA_HEAD_DIM), jnp.bfloat16),
        compiler_params=_params(2),
        name="window",
    )(sink, q, k, k, k, k_lead, v, v, v, v_lead)


def _mixout_kernel(x_ref, oa_ref, ob_ref, pre_g_ref, wg_ref, woa_ref, wob_ref, wout_ref,
                   post_g_ref, o_ref):
    x1 = x_ref[0]
    h = _rms(x1, pre_g_ref[...]).astype(jnp.bfloat16)
    y_a = _dot(oa_ref[0], woa_ref[...])
    y_b = _dot(ob_ref[0], wob_ref[...])
    merged = (jax.nn.sigmoid(_dot(h, wg_ref[:, :D_MODEL])) * y_a
              + jax.nn.sigmoid(_dot(h, wg_ref[:, D_MODEL:])) * y_b)
    o_ref[0] = x1 + _rms(_dot(merged.astype(jnp.bfloat16), wout_ref[...]), post_g_ref[...])


def _mixout(x1, o_a, o_b, pre_g, wg, woa, wob, wout, post_g, tm):
    b, s, d = x1.shape
    row = pl.BlockSpec((1, tm, d), lambda i, t: (i, t, 0))
    half = pl.BlockSpec((1, tm, o_a.shape[2]), lambda i, t: (i, t, 0))
    return pl.pallas_call(
        _mixout_kernel,
        grid=(b, s // tm),
        in_specs=[row, half, half, _const_spec(pre_g.shape), _const_spec(wg.shape),
                  _const_spec(woa.shape), _const_spec(wob.shape), _const_spec(wout.shape),
                  _const_spec(post_g.shape)],
        out_specs=row,
        out_shape=jax.ShapeDtypeStruct(x1.shape, jnp.float32),
        compiler_params=_params(2),
        name="mixout",
    )(x1, o_a, o_b, pre_g, wg, woa, wob, wout, post_g)


def _rot_cols(w, n_heads, dim):
    w = w.reshape(w.shape[0], n_heads, dim)
    half = dim // 2
    return jnp.concatenate([-w[..., half:], w[..., :half]], axis=-1).reshape(w.shape[0], -1)


def _pack_weights(w_in, w_uq, w_ukv):
    bf16 = jnp.bfloat16
    d = w_in.shape[0]
    o = 0
    parts = []
    for n in (MLA_Q_LORA, MLA_KV_LORA, MLA_ROPE, SWA_HEADS * SWA_HEAD_DIM,
              SWA_KV_HEADS * SWA_HEAD_DIM, SWA_KV_HEADS * SWA_HEAD_DIM, D_MODEL, D_MODEL):
        parts.append(w_in[:, o:o + n])
        o += n
    w_cq, w_ckv, w_kr, w_qs, w_ks, w_vs, w_ga, w_gb = parts

    def rope_slab(w):
        return jnp.concatenate([jnp.zeros((d, MLA_NOPE), w.dtype), w,
                                jnp.zeros((d, LANES - MLA_NOPE - MLA_ROPE), w.dtype)], axis=1)

    w_proj = jnp.concatenate(
        [w_cq, w_ckv, w_qs, _rot_cols(w_qs, SWA_HEADS, SWA_HEAD_DIM),
         w_ks, _rot_cols(w_ks, SWA_KV_HEADS, SWA_HEAD_DIM), w_vs,
         rope_slab(w_kr), rope_slab(_rot_cols(w_kr, 1, MLA_ROPE))], axis=1).astype(bf16)
    w_gate = jnp.concatenate([w_ga, w_gb], axis=1).astype(bf16)

    r = w_uq.shape[0]
    uq = w_uq.reshape(r, MLA_HEADS, MLA_NOPE + MLA_ROPE)
    pad = jnp.zeros((r, MLA_HEADS, LANES - MLA_NOPE - MLA_ROPE), w_uq.dtype)
    q_plain = jnp.concatenate([uq, pad], axis=-1)
    q_rot = jnp.concatenate(
        [jnp.zeros((r, MLA_HEADS, MLA_NOPE), w_uq.dtype),
         _rot_cols(uq[..., MLA_NOPE:].reshape(r, -1), MLA_HEADS, MLA_ROPE).reshape(
             r, MLA_HEADS, MLA_ROPE), pad], axis=-1)
    wq = jnp.concatenate([q_plain.reshape(r, -1), q_rot.reshape(r, -1)], axis=1).astype(bf16)

    r = w_ukv.shape[0]
    ukv = w_ukv.reshape(r, MLA_HEADS, MLA_NOPE + MLA_V)
    zeros = jnp.zeros((r, MLA_HEADS, LANES - MLA_NOPE), w_ukv.dtype)
    k_slab = jnp.concatenate([ukv[..., :MLA_NOPE], zeros], axis=-1)
    v_h = ukv[..., MLA_NOPE:]
    zv = jnp.zeros_like(v_h)
    odd = (jnp.arange(MLA_HEADS) % 2 == 1)[None, :, None]
    v_slab = jnp.concatenate([jnp.where(odd, zv, v_h), jnp.where(odd, v_h, zv)], axis=-1)
    wkv = jnp.concatenate([k_slab.reshape(r, -1), v_slab.reshape(r, -1)], axis=1).astype(bf16)
    return w_proj, w_gate, wq, wkv


def _rope_tables(pos):
    pos = pos.astype(jnp.float32)[:, None]
    lane = jnp.arange(LANES)

    def inv_freq(half):
        return ROPE_THETA ** (-jnp.arange(half, dtype=jnp.float32) / half)

    half_k = MLA_ROPE // 2
    ang_k = pos * inv_freq(half_k)[(lane - MLA_NOPE) % half_k][None, :]
    is_rope = jnp.logical_and(lane >= MLA_NOPE, lane < MLA_NOPE + MLA_ROPE)[None, :]
    ck = jnp.where((lane < MLA_NOPE)[None, :], 1.0, jnp.where(is_rope, jnp.cos(ang_k), 0.0))
    sk = jnp.where(is_rope, jnp.sin(ang_k), 0.0)
    half_w = SWA_HEAD_DIM // 2
    ang_w = pos * inv_freq(half_w)[lane % half_w][None, :]
    return ck, sk, jnp.cos(ang_w), jnp.sin(ang_w)


def _row_tile(s):
    return min(s, 512)


def kernel(x_prompt, x_sample, meta_tokens, ffn1_pre_g, ffn1_w_in, ffn1_w_out, ffn1_post_g,
           mix_pre_g, w_in, q_norm_g, w_uq, kv_norm_g, w_ukv, sink, w_o_a, w_o_b, w_out,
           mix_post_g, ffn2_pre_g, ffn2_w_in, ffn2_w_out, ffn2_post_g):
    assert ffn1_w_in.shape[0] == 1, "single layer"
    bf16 = jnp.bfloat16
    f1 = (ffn1_pre_g, ffn1_w_in[0].astype(bf16), ffn1_w_out[0].astype(bf16), ffn1_post_g)
    f2 = (ffn2_pre_g, ffn2_w_in[0].astype(bf16), ffn2_w_out[0].astype(bf16), ffn2_post_g)
    w_proj, w_gate, wq, wkv = _pack_weights(w_in[0], w_uq[0], w_ukv[0])
    woa, wob, wout = w_o_a[0].astype(bf16), w_o_b[0].astype(bf16), w_out[0].astype(bf16)
    sink = sink[0]

    def front(x, tabs):
        tm = _row_tile(x.shape[1])
        x1 = _ffn(x, *f1, tm)
        return x1, _proj(x1, tabs, mix_pre_g, w_proj, q_norm_g, wq, kv_norm_g, wkv, tm)

    lead = jnp.concatenate([jnp.zeros((LEAD_PAD, D_MODEL), x_prompt.dtype),
                            meta_tokens.astype(x_prompt.dtype)], axis=0)[None]
    _, (_, ka_lead, va_lead, _, kw_lead, vw_lead) = front(
        lead, _rope_tables(jnp.arange(BLOCK) - LEAD_PAD))
    seq_tabs = _rope_tables(jnp.arange(max(x_prompt.shape[1], x_sample.shape[1])) + N_META)

    def trunk(x):
        s = x.shape[1]
        tm = _row_tile(s)
        x1, (qa, ka, va, qw, kw, vw) = front(x, seq_tabs)
        o_a = _mla(qa, ka, va, ka_lead, va_lead, tq=256, ck=2048)
        o_b = _window(sink, qw, kw, vw, kw_lead, vw_lead, tq=4 * BLOCK)
        x2 = _mixout(x1, o_a, o_b, mix_pre_g, w_gate, woa, wob, wout, mix_post_g, tm)
        return _ffn(x2, *f2, tm)

    return (trunk(x_prompt), trunk(x_sample))
```

```python
import functools

import jax
import jax.numpy as jnp
from jax import lax
from jax.experimental import pallas as pl
from jax.experimental.pallas import tpu as pltpu

D_MODEL = 1024
N_META = 16
BLOCK = 128
WINDOW = 128
LEAD_PAD = BLOCK - N_META
ROPE_THETA = 10000.0
EPS = 1e-6
NEG = -1e30
MLA_HEADS = 8
MLA_Q_LORA = 384
MLA_KV_LORA = 256
MLA_NOPE = 64
MLA_ROPE = 32
MLA_V = 64
SWA_HEADS = 8
SWA_KV_HEADS = 2
SWA_HEAD_DIM = 64
SWA_GROUP = SWA_HEADS // SWA_KV_HEADS
D_FF = 2816

LANES = 128
MLA_SCALE = (MLA_NOPE + MLA_ROPE) ** -0.5
SWA_SCALE = SWA_HEAD_DIM ** -0.5
V7X_VMEM_BYTES = 64 * 1024 * 1024
VMEM_LIMIT = (V7X_VMEM_BYTES * 7) // 8

_C_Q = (0, MLA_Q_LORA)
_C_KV = (_C_Q[1], _C_Q[1] + MLA_KV_LORA)
_C_QS = (_C_KV[1], _C_KV[1] + SWA_HEADS * SWA_HEAD_DIM)
_C_QS_ROT = (_C_QS[1], _C_QS[1] + SWA_HEADS * SWA_HEAD_DIM)
_C_KS = (_C_QS_ROT[1], _C_QS_ROT[1] + LANES)
_C_KS_ROT = (_C_KS[1], _C_KS[1] + LANES)
_C_VS = (_C_KS_ROT[1], _C_KS_ROT[1] + LANES)
_C_KR = (_C_VS[1], _C_VS[1] + LANES)
_C_KR_ROT = (_C_KR[1], _C_KR[1] + LANES)
D_PROJ = _C_KR_ROT[1]

MXU_COLS = 256
FFN_CHUNKS = ((0, 6 * MXU_COLS), (6 * MXU_COLS, D_FF))
assert all((hi - lo) % MXU_COLS == 0 for lo, hi in FFN_CHUNKS)


def _const_spec(shape):
    zeros = (0,) * len(shape)
    return pl.BlockSpec(shape, lambda *_: zeros, pipeline_mode=pl.Buffered(1))


def _params(n_axes):
    return pltpu.CompilerParams(
        dimension_semantics=("parallel",) * n_axes, vmem_limit_bytes=VMEM_LIMIT)


def _rms(x, g):
    y = x * lax.rsqrt(jnp.mean(x * x, axis=-1, keepdims=True) + EPS)
    return y * g


def _dot(a, b):
    return jnp.dot(a, b, preferred_element_type=jnp.float32)


def _dot_nt(a, b):
    return lax.dot_general(a, b, (((1,), (1,)), ((), ())),
                           preferred_element_type=jnp.float32)


def _swiglu_residual(x, pre_g, w_in_ref, w_out_ref, post_g):
    h = _rms(x, pre_g).astype(jnp.bfloat16)
    acc = None
    for lo, hi in FFN_CHUNKS:
        g = _dot(h, w_in_ref[:, lo:hi])
        u = _dot(h, w_in_ref[:, D_FF + lo:D_FF + hi])
        a = (g * jax.nn.sigmoid(g) * u).astype(jnp.bfloat16)
        part = _dot(a, w_out_ref[lo:hi, :])
        acc = part if acc is None else acc + part
    return x + 0.5 * _rms(acc, post_g)


def _ffn_kernel(x_ref, pre_g_ref, w_in_ref, w_out_ref, post_g_ref, o_ref):
    o_ref[0] = _swiglu_residual(x_ref[0], pre_g_ref[...], w_in_ref, w_out_ref,
                                post_g_ref[...])


def _ffn(x, pre_g, w_in, w_out, post_g, tm):
    b, s, d = x.shape
    row = pl.BlockSpec((1, tm, d), lambda i, t: (i, t, 0))
    return pl.pallas_call(
        _ffn_kernel,
        grid=(b, s // tm),
        in_specs=[row, _const_spec(pre_g.shape), _const_spec(w_in.shape),
                  _const_spec(w_out.shape), _const_spec(post_g.shape)],
        out_specs=row,
        out_shape=jax.ShapeDtypeStruct(x.shape, jnp.float32),
        compiler_params=_params(2),
        name="ffn",
    )(x, pre_g, w_in, w_out, post_g)


def _proj_kernel(x_ref, ck_ref, sk_ref, cw_ref, sw_ref, pre_g_ref, w_ref, qn_g_ref, wq_ref,
                 kvn_g_ref, wk_ref, wvt_ref, qa_ref, ka_ref, vt_ref, qw_ref, kw_ref, vw_ref):
    bf16 = jnp.bfloat16
    h = _rms(x_ref[0], pre_g_ref[...]).astype(bf16)
    ck_t, sk_t, cw_t, sw_t = ck_ref[...], sk_ref[...], cw_ref[...], sw_ref[...]
    lane = lax.broadcasted_iota(jnp.int32, (1, LANES), 1)
    low_half = lane < SWA_HEAD_DIM

    proj = _dot(h, w_ref[...])

    def cols(rng):
        return proj[:, rng[0]:rng[1]]

    qn = _rms(cols(_C_Q), qn_g_ref[...]).astype(bf16)
    qq = _dot(qn, wq_ref[...])
    n_q = MLA_HEADS * LANES
    for hd in range(MLA_HEADS):
        sl = slice(hd * LANES, (hd + 1) * LANES)
        q_rope = qq[:, sl] * ck_t + qq[:, n_q + hd * LANES:n_q + (hd + 1) * LANES] * sk_t
        qa_ref[0, hd] = (q_rope * (MLA_SCALE * LOG2_E)).astype(bf16)

    kvn = _rms(cols(_C_KV), kvn_g_ref[...]).astype(bf16)
    kk = _dot(kvn, wk_ref[...])
    k_rope = cols(_C_KR) * ck_t + cols(_C_KR_ROT) * sk_t
    for hd in range(MLA_HEADS):
        ka_ref[0, hd] = (kk[:, hd * LANES:(hd + 1) * LANES] + k_rope).astype(bf16)

    vt = _dot_nt(wvt_ref[...], kvn)
    is_v_row = lax.broadcasted_iota(jnp.int32, (LANES, 1), 0) < MLA_V
    for hd in range(MLA_HEADS):
        vt_ref[0, hd] = jnp.where(is_v_row, vt[hd * LANES:(hd + 1) * LANES], 1.0).astype(bf16)

    qs = cols(_C_QS)
    qs_rot = cols(_C_QS_ROT)
    for pair in range(SWA_HEADS // 2):
        sl = slice(pair * LANES, (pair + 1) * LANES)
        q_pair = (qs[:, sl] * cw_t + qs_rot[:, sl] * sw_t) * (SWA_SCALE * LOG2_E)
        qw_ref[0, 2 * pair] = jnp.where(low_half, q_pair, 0.0).astype(bf16)
        qw_ref[0, 2 * pair + 1] = jnp.where(low_half, 0.0, q_pair).astype(bf16)

    k_pair = cols(_C_KS) * cw_t + cols(_C_KS_ROT) * sw_t
    k_swapped = pltpu.roll(k_pair, SWA_HEAD_DIM, 1)
    kw_ref[0, 0] = jnp.where(low_half, k_pair, k_swapped).astype(bf16)
    kw_ref[0, 1] = jnp.where(low_half, k_swapped, k_pair).astype(bf16)

    v_pair = cols(_C_VS)
    v_swapped = pltpu.roll(v_pair, SWA_HEAD_DIM, 1)
    vw_ref[0, 0, :, :LANES] = jnp.where(low_half, v_pair, 1.0).astype(bf16)
    vw_ref[0, 0, :, LANES:] = jnp.where(low_half, 1.0, v_swapped).astype(bf16)
    vw_ref[0, 1, :, :LANES] = jnp.where(low_half, v_swapped, 1.0).astype(bf16)
    vw_ref[0, 1, :, LANES:] = jnp.where(low_half, 1.0, v_pair).astype(bf16)


def _proj(x1, tabs, pre_g, w_proj, qn_g, wq, kvn_g, wk, wvt, tm):
    b, s, d = x1.shape
    row = pl.BlockSpec((1, tm, d), lambda i, t: (i, t, 0))
    tab_spec = pl.BlockSpec((tm, LANES), lambda i, t: (t, 0))

    def heads(n, lanes):
        return (pl.BlockSpec((1, n, tm, lanes), lambda i, t: (i, 0, t, 0)),
                jax.ShapeDtypeStruct((b, n, s, lanes), jnp.bfloat16))

    vt_out = (pl.BlockSpec((1, MLA_HEADS, LANES, tm), lambda i, t: (i, 0, 0, t)),
              jax.ShapeDtypeStruct((b, MLA_HEADS, LANES, s), jnp.bfloat16))
    outs = (heads(MLA_HEADS, LANES), heads(MLA_HEADS, LANES), vt_out,
            heads(SWA_HEADS, LANES), heads(SWA_KV_HEADS, LANES),
            heads(SWA_KV_HEADS, WIN_V_LANES))
    consts = (pre_g, w_proj, qn_g, wq, kvn_g, wk, wvt)
    return pl.pallas_call(
        _proj_kernel,
        grid=(b, s // tm),
        in_specs=[row] + [tab_spec] * len(tabs) + [_const_spec(c.shape) for c in consts],
        out_specs=[o[0] for o in outs],
        out_shape=[o[1] for o in outs],
        compiler_params=_params(2),
        name="proj",
    )(x1, *tabs, *consts)


LOG2_E = 1.4426950408889634


def _mla_kernel(q_ref, k_ref, vt_ref, kl_ref, vtl_ref, o_ref, s_a, s_b, ot_scr, *, tq, ck):
    seq = k_ref.shape[2]
    n_chunks = seq // ck
    n_tiles = seq // tq
    sublanes = 8
    lead_ok = lax.broadcasted_iota(jnp.int32, (BLOCK, 1), 0) >= LEAD_PAD

    def fold_max(s):
        return jnp.max(s.reshape(s.shape[0] // sublanes, sublanes, tq), axis=0)

    def rows(t):
        return pl.ds(pl.multiple_of(t * tq, tq), tq)

    def score_chunk(j, q, c, s_new, mx):
        s = _dot_nt(k_ref[0, j, c * ck:(c + 1) * ck, :], q)
        s_new[c] = s
        return jnp.maximum(mx, fold_max(s))

    def value_chunk(j, c, s_old, m, acc):
        p = jnp.exp2(s_old[c] - m).astype(jnp.bfloat16)
        return acc + _dot(vt_ref[0, j, :, c * ck:(c + 1) * ck], p)

    def emit(j, t, acc):
        o_t = acc[:MLA_V] * (1.0 / acc[MLA_V:MLA_V + 1])
        if j % 2 == 0:
            ot_scr[t] = o_t
        else:
            both = jnp.concatenate([ot_scr[t], o_t], axis=0)
            slab = slice((j // 2) * LANES, (j // 2 + 1) * LANES)
            o_ref[0, rows(t), slab] = both.T.astype(o_ref.dtype)

    def stage(new, old, s_new, s_old, m_prev, acc_prev):
        j, t = new
        q = q_ref[0, j, rows(t), :]
        s_lead = jnp.where(lead_ok, _dot_nt(kl_ref[0, j], q), NEG)
        mx, acc = fold_max(s_lead), acc_prev
        for c in range(n_chunks):
            mx = score_chunk(j, q, c, s_new, mx)
            if old is not None:
                acc = value_chunk(old[0], c, s_old, m_prev, acc)
        if old is not None:
            emit(*old, acc)
        m = jnp.max(mx, axis=0, keepdims=True)
        return m, _dot(vtl_ref[0, j], jnp.exp2(s_lead - m).astype(jnp.bfloat16))

    carry = stage((0, 0), None, s_a, None, None, None)
    n_heads = q_ref.shape[1]
    for j in range(n_heads):

        def pair(i, carry, j=j):
            m, acc = stage((j, 2 * i + 1), (j, 2 * i), s_b, s_a, *carry)
            return stage((j, 2 * i + 2), (j, 2 * i + 1), s_a, s_b, m, acc)

        carry = lax.fori_loop(0, n_tiles // 2 - 1, pair, carry)
        carry = stage((j, n_tiles - 1), (j, n_tiles - 2), s_b, s_a, *carry)
        if j + 1 < n_heads:
            carry = stage((j + 1, 0), (j, n_tiles - 1), s_a, s_b, *carry)
    m, acc = carry
    last = n_heads - 1
    for c in range(n_chunks):
        acc = value_chunk(last, c, s_b, m, acc)
    emit(last, n_tiles - 1, acc)


def _mla_heads_per_step(nh, s, tq, ck):
    bf16_bytes, f32_bytes, pipeline_buffers = 2, 4, 2
    scratch = 2 * s * tq * f32_bytes + s * MLA_V * f32_bytes
    in_flight = 2 * ck * tq * f32_bytes
    for hp in range(nh, 0, -2):
        per_head = (3 * s * LANES + s * MLA_V) * bf16_bytes
        if nh % hp == 0 and pipeline_buffers * hp * per_head + scratch + in_flight <= VMEM_LIMIT:
            return hp
    raise ValueError("sequence too long for the resident-key MLA kernel")


def _mla(q, k, vt, k_lead, vt_lead, tq, ck):
    b, nh, s, _ = q.shape
    hp = _mla_heads_per_step(nh, s, tq, ck)
    assert s % ck == 0 and s % (2 * tq) == 0
    seq_spec = pl.BlockSpec((1, hp, s, LANES), lambda i, p: (i, p, 0, 0))
    vt_spec = pl.BlockSpec((1, hp, LANES, s), lambda i, p: (i, p, 0, 0))
    lead_spec = pl.BlockSpec((1, hp, BLOCK, LANES), lambda i, p: (0, p, 0, 0))
    score_buf = pltpu.VMEM((s // ck, ck, tq), jnp.float32)
    return pl.pallas_call(
        functools.partial(_mla_kernel, tq=tq, ck=ck),
        grid=(b, nh // hp),
        in_specs=[seq_spec, seq_spec, vt_spec, lead_spec, lead_spec],
        out_specs=pl.BlockSpec((1, s, hp * MLA_V), lambda i, p: (i, 0, p)),
        out_shape=jax.ShapeDtypeStruct((b, s, nh * MLA_V), jnp.bfloat16),
        scratch_shapes=[score_buf, score_buf,
                        pltpu.VMEM((s // tq, MLA_V, tq), jnp.float32)],
        compiler_params=_params(2),
        name="mla",
    )(q, k, vt, k_lead, vt_lead)


WIN_V_LANES = 2 * LANES


def _window_kernel(sink_ref, q_ref, kl_ref, kc_ref, kr_ref, km_ref,
                   vl_ref, vc_ref, vr_ref, vm_ref, o_ref):
    tq = q_ref.shape[2]
    n_blk = tq // BLOCK
    t = pl.program_id(1)
    n_t = pl.num_programs(1)
    n_keys = 4 * BLOCK
    row = lax.broadcasted_iota(jnp.int32, (BLOCK, n_keys), 0)
    col = lax.broadcasted_iota(jnp.int32, (BLOCK, n_keys), 1)
    in_left = jnp.logical_and(col < BLOCK, col >= row)
    in_right = jnp.logical_and(jnp.logical_and(col >= 2 * BLOCK, col < 3 * BLOCK),
                               col - 2 * BLOCK <= row)
    in_rest = jnp.logical_or(jnp.logical_and(col >= BLOCK, col < 2 * BLOCK),
                             col >= 3 * BLOCK + LEAD_PAD)
    low_half = lax.broadcasted_iota(jnp.int32, (1, LANES), 1) < SWA_HEAD_DIM

    def keys_of(blk, left_ref, center_ref, right_ref, lead_ref, g):
        first, last = blk == 0, blk == n_blk - 1
        return jnp.concatenate(
            [left_ref[0, g] if first else center_ref[0, g, (blk - 1) * BLOCK:blk * BLOCK, :],
             center_ref[0, g, blk * BLOCK:(blk + 1) * BLOCK, :],
             right_ref[0, g] if last else center_ref[0, g, (blk + 1) * BLOCK:(blk + 2) * BLOCK, :],
             lead_ref[0, g]], axis=0)

    def scores(g, blk):
        ok_left = jnp.logical_and(in_left, t > 0) if blk == 0 else in_left
        ok_right = jnp.logical_and(in_right, t < n_t - 1) if blk == n_blk - 1 else in_right
        ok = jnp.logical_or(jnp.logical_or(ok_left, ok_right), in_rest)
        q = q_ref[0, g * SWA_GROUP:(g + 1) * SWA_GROUP, blk * BLOCK:(blk + 1) * BLOCK, :]
        s = _dot_nt(q.reshape(SWA_GROUP * BLOCK, LANES),
                    keys_of(blk, kl_ref, kc_ref, kr_ref, km_ref, g))
        return jnp.where(ok[None], s.reshape(SWA_GROUP, BLOCK, n_keys), NEG)

    def finish(g, blk, s):
        sink = jnp.concatenate(
            [jnp.full((1, BLOCK, 1), sink_ref[g * SWA_GROUP + r] * LOG2_E, jnp.float32)
             for r in range(SWA_GROUP)], axis=0)
        m = jnp.maximum(jnp.max(s, axis=-1, keepdims=True), sink)
        p = jnp.exp2(s - m).astype(jnp.bfloat16).reshape(SWA_GROUP * BLOCK, n_keys)
        tail = jnp.exp2(sink - m)
        acc = _dot(p, keys_of(blk, vl_ref, vc_ref, vr_ref, vm_ref, g)).reshape(
            SWA_GROUP, BLOCK, WIN_V_LANES)
        for pair in range(SWA_GROUP // 2):
            even, odd = acc[2 * pair], acc[2 * pair + 1]
            num = jnp.where(low_half, even[:, :LANES], odd[:, LANES:])
            den = (jnp.where(low_half, even[:, LANES:], odd[:, :LANES])
                   + jnp.where(low_half, tail[2 * pair], tail[2 * pair + 1]))
            slab = g * (SWA_GROUP // 2) + pair
            o_ref[0, blk * BLOCK:(blk + 1) * BLOCK, slab * LANES:(slab + 1) * LANES] = (
                num * (1.0 / den)).astype(o_ref.dtype)

    units = [(g, blk) for g in range(SWA_KV_HEADS) for blk in range(n_blk)]
    s_next = scores(*units[0])
    for i, unit in enumerate(units):
        s_cur = s_next
        if i + 1 < len(units):
            s_next = scores(*units[i + 1])
        finish(*unit, s_cur)


def _window(sink, q, k, v, k_lead, v_lead, tq):
    b, nh, s, _ = q.shape
    n_blk = s // BLOCK
    per = tq // BLOCK

    def specs(n, lanes):
        left = pl.BlockSpec((1, n, BLOCK, lanes),
                            lambda i, t: (i, 0, jnp.maximum(t * per - 1, 0), 0))
        center = pl.BlockSpec((1, n, tq, lanes), lambda i, t: (i, 0, t, 0))
        right = pl.BlockSpec((1, n, BLOCK, lanes),
                             lambda i, t: (i, 0, jnp.minimum((t + 1) * per, n_blk - 1), 0))
        lead = pl.BlockSpec((1, n, BLOCK, lanes), lambda i, t: (0, 0, 0, 0))
        return [left, center, right, lead]

    smem = pl.BlockSpec(memory_space=pltpu.SMEM)
    return pl.pallas_call(
        _window_kernel,
        grid=(b, s // tq),
        in_specs=([smem, specs(nh, LANES)[1]] + specs(SWA_KV_HEADS, LANES)
                  + specs(SWA_KV_HEADS, WIN_V_LANES)),
        out_specs=pl.BlockSpec((1, tq, nh * SWA_HEAD_DIM), lambda i, t: (i, t, 0)),
        out_shape=jax.ShapeDtypeStruct((b, s, nh * SWA_HEAD_DIM), jnp.bfloat16),
        compiler_params=_params(2),
        name="window",
    )(sink, q, k, k, k, k_lead, v, v, v, v_lead)


def _mixout_kernel(x_ref, oa_ref, ob_ref, pre_g_ref, wg_ref, woa_ref, wob_ref, wout_ref,
                   post_g_ref, o_ref):
    x1 = x_ref[0]
    h = _rms(x1, pre_g_ref[...]).astype(jnp.bfloat16)
    y_a = _dot(oa_ref[0], woa_ref[...])
    y_b = _dot(ob_ref[0], wob_ref[...])
    merged = (jax.nn.sigmoid(_dot(h, wg_ref[:, :D_MODEL])) * y_a
              + jax.nn.sigmoid(_dot(h, wg_ref[:, D_MODEL:])) * y_b)
    o_ref[0] = x1 + _rms(_dot(merged.astype(jnp.bfloat16), wout_ref[...]), post_g_ref[...])


def _mixout(x1, o_a, o_b, pre_g, wg, woa, wob, wout, post_g, tm):
    b, s, d = x1.shape
    row = pl.BlockSpec((1, tm, d), lambda i, t: (i, t, 0))
    half = pl.BlockSpec((1, tm, o_a.shape[2]), lambda i, t: (i, t, 0))
    return pl.pallas_call(
        _mixout_kernel,
        grid=(b, s // tm),
        in_specs=[row, half, half, _const_spec(pre_g.shape), _const_spec(wg.shape),
                  _const_spec(woa.shape), _const_spec(wob.shape), _const_spec(wout.shape),
                  _const_spec(post_g.shape)],
        out_specs=row,
        out_shape=jax.ShapeDtypeStruct(x1.shape, jnp.float32),
        compiler_params=_params(2),
        name="mixout",
    )(x1, o_a, o_b, pre_g, wg, woa, wob, wout, post_g)


def _rot_cols(w, n_heads, dim):
    w = w.reshape(w.shape[0], n_heads, dim)
    half = dim // 2
    return jnp.concatenate([-w[..., half:], w[..., :half]], axis=-1).reshape(w.shape[0], -1)


def _pack_weights(w_in, w_uq, w_ukv):
    bf16 = jnp.bfloat16
    d = w_in.shape[0]
    o = 0
    parts = []
    for n in (MLA_Q_LORA, MLA_KV_LORA, MLA_ROPE, SWA_HEADS * SWA_HEAD_DIM,
              SWA_KV_HEADS * SWA_HEAD_DIM, SWA_KV_HEADS * SWA_HEAD_DIM, D_MODEL, D_MODEL):
        parts.append(w_in[:, o:o + n])
        o += n
    w_cq, w_ckv, w_kr, w_qs, w_ks, w_vs, w_ga, w_gb = parts

    def rope_slab(w):
        return jnp.concatenate([jnp.zeros((d, MLA_NOPE), w.dtype), w,
                                jnp.zeros((d, LANES - MLA_NOPE - MLA_ROPE), w.dtype)], axis=1)

    w_proj = jnp.concatenate(
        [w_cq, w_ckv, w_qs, _rot_cols(w_qs, SWA_HEADS, SWA_HEAD_DIM),
         w_ks, _rot_cols(w_ks, SWA_KV_HEADS, SWA_HEAD_DIM), w_vs,
         rope_slab(w_kr), rope_slab(_rot_cols(w_kr, 1, MLA_ROPE))], axis=1).astype(bf16)
    w_gate = jnp.concatenate([w_ga, w_gb], axis=1).astype(bf16)

    r = w_uq.shape[0]
    uq = w_uq.reshape(r, MLA_HEADS, MLA_NOPE + MLA_ROPE)
    pad = jnp.zeros((r, MLA_HEADS, LANES - MLA_NOPE - MLA_ROPE), w_uq.dtype)
    q_plain = jnp.concatenate([uq, pad], axis=-1)
    q_rot = jnp.concatenate(
        [jnp.zeros((r, MLA_HEADS, MLA_NOPE), w_uq.dtype),
         _rot_cols(uq[..., MLA_NOPE:].reshape(r, -1), MLA_HEADS, MLA_ROPE).reshape(
             r, MLA_HEADS, MLA_ROPE), pad], axis=-1)
    wq = jnp.concatenate([q_plain.reshape(r, -1), q_rot.reshape(r, -1)], axis=1).astype(bf16)

    r = w_ukv.shape[0]
    ukv = w_ukv.reshape(r, MLA_HEADS, MLA_NOPE + MLA_V)
    zeros = jnp.zeros((r, MLA_HEADS, LANES - MLA_NOPE), w_ukv.dtype)
    k_slab = jnp.concatenate([ukv[..., :MLA_NOPE], zeros], axis=-1)
    wk = k_slab.reshape(r, -1).astype(bf16)
    v_slab = jnp.concatenate([ukv[..., MLA_NOPE:], zeros], axis=-1)
    wvt = v_slab.reshape(r, -1).T.astype(bf16)
    return w_proj, w_gate, wq, wk, wvt


def _rope_tables(pos):
    pos = pos.astype(jnp.float32)[:, None]
    lane = jnp.arange(LANES)

    def inv_freq(half):
        return ROPE_THETA ** (-jnp.arange(half, dtype=jnp.float32) / half)

    half_k = MLA_ROPE // 2
    ang_k = pos * inv_freq(half_k)[(lane - MLA_NOPE) % half_k][None, :]
    is_rope = jnp.logical_and(lane >= MLA_NOPE, lane < MLA_NOPE + MLA_ROPE)[None, :]
    ck = jnp.where((lane < MLA_NOPE)[None, :], 1.0, jnp.where(is_rope, jnp.cos(ang_k), 0.0))
    sk = jnp.where(is_rope, jnp.sin(ang_k), 0.0)
    half_w = SWA_HEAD_DIM // 2
    ang_w = pos * inv_freq(half_w)[lane % half_w][None, :]
    return ck, sk, jnp.cos(ang_w), jnp.sin(ang_w)


def _row_tile(s):
    return min(s, 512)


def kernel(x_prompt, x_sample, meta_tokens, ffn1_pre_g, ffn1_w_in, ffn1_w_out, ffn1_post_g,
           mix_pre_g, w_in, q_norm_g, w_uq, kv_norm_g, w_ukv, sink, w_o_a, w_o_b, w_out,
           mix_post_g, ffn2_pre_g, ffn2_w_in, ffn2_w_out, ffn2_post_g):
    assert ffn1_w_in.shape[0] == 1, "single layer"
    bf16 = jnp.bfloat16
    f1 = (ffn1_pre_g, ffn1_w_in[0].astype(bf16), ffn1_w_out[0].astype(bf16), ffn1_post_g)
    f2 = (ffn2_pre_g, ffn2_w_in[0].astype(bf16), ffn2_w_out[0].astype(bf16), ffn2_post_g)
    w_proj, w_gate, wq, wk, wvt = _pack_weights(w_in[0], w_uq[0], w_ukv[0])
    woa, wob, wout = w_o_a[0].astype(bf16), w_o_b[0].astype(bf16), w_out[0].astype(bf16)
    sink = sink[0]

    def front(x, tabs):
        tm = _row_tile(x.shape[1])
        x1 = _ffn(x, *f1, tm)
        return x1, _proj(x1, tabs, mix_pre_g, w_proj, q_norm_g, wq, kv_norm_g, wk, wvt,
                         tm)

    lead = jnp.concatenate([jnp.zeros((LEAD_PAD, D_MODEL), x_prompt.dtype),
                            meta_tokens.astype(x_prompt.dtype)], axis=0)[None]
    _, (_, ka_lead, va_lead, _, kw_lead, vw_lead) = front(
        lead, _rope_tables(jnp.arange(BLOCK) - LEAD_PAD))
    seq_tabs = _rope_tables(jnp.arange(max(x_prompt.shape[1], x_sample.shape[1])) + N_META)

    def trunk(x):
        s = x.shape[1]
        tm = _row_tile(s)
        x1, (qa, ka, va, qw, kw, vw) = front(x, seq_tabs)
        o_a = _mla(qa, ka, va, ka_lead, va_lead, tq=MXU_COLS, ck=2048)
        o_b = _window(sink, qw, kw, vw, kw_lead, vw_lead, tq=4 * BLOCK)
        x2 = _mixout(x1, o_a, o_b, mix_pre_g, w_gate, woa, wob, wout, mix_post_g, tm)
        return _ffn(x2, *f2, tm)

    return (trunk(x_prompt), trunk(x_sample))
```

```python
import functools

import jax
import jax.numpy as jnp
from jax import lax
from jax.experimental import pallas as pl
from jax.experimental.pallas import tpu as pltpu

D_MODEL = 1024
N_META = 16
BLOCK = 128
WINDOW = 128
LEAD_PAD = BLOCK - N_META
ROPE_THETA = 10000.0
EPS = 1e-6
NEG = -1e30
MLA_HEADS = 8
MLA_Q_LORA = 384
MLA_KV_LORA = 256
MLA_NOPE = 64
MLA_ROPE = 32
MLA_V = 64
SWA_HEADS = 8
SWA_KV_HEADS = 2
SWA_HEAD_DIM = 64
SWA_GROUP = SWA_HEADS // SWA_KV_HEADS
D_FF = 2816

LANES = 128
MLA_SCALE = (MLA_NOPE + MLA_ROPE) ** -0.5
SWA_SCALE = SWA_HEAD_DIM ** -0.5
V7X_VMEM_BYTES = 64 * 1024 * 1024
VMEM_LIMIT = (V7X_VMEM_BYTES * 7) // 8

_C_Q = (0, MLA_Q_LORA)
_C_KV = (_C_Q[1], _C_Q[1] + MLA_KV_LORA)
_C_QS = (_C_KV[1], _C_KV[1] + SWA_HEADS * SWA_HEAD_DIM)
_C_QS_ROT = (_C_QS[1], _C_QS[1] + SWA_HEADS * SWA_HEAD_DIM)
_C_KS = (_C_QS_ROT[1], _C_QS_ROT[1] + LANES)
_C_KS_ROT = (_C_KS[1], _C_KS[1] + LANES)
_C_KR = (_C_KS_ROT[1], _C_KS_ROT[1] + LANES)
_C_KR_ROT = (_C_KR[1], _C_KR[1] + LANES)
D_PROJ = _C_KR_ROT[1]

MXU_COLS = 256
FFN_CHUNKS = ((0, 6 * MXU_COLS), (6 * MXU_COLS, D_FF))
assert all((hi - lo) % MXU_COLS == 0 for lo, hi in FFN_CHUNKS)


def _const_spec(shape):
    zeros = (0,) * len(shape)
    return pl.BlockSpec(shape, lambda *_: zeros, pipeline_mode=pl.Buffered(1))


def _params(n_axes):
    return pltpu.CompilerParams(
        dimension_semantics=("parallel",) * n_axes, vmem_limit_bytes=VMEM_LIMIT)


def _rms(x, g):
    y = x * lax.rsqrt(jnp.mean(x * x, axis=-1, keepdims=True) + EPS)
    return y * g


def _dot(a, b):
    return jnp.dot(a, b, preferred_element_type=jnp.float32)


def _dot_nt(a, b):
    return lax.dot_general(a, b, (((1,), (1,)), ((), ())),
                           preferred_element_type=jnp.float32)


def _swiglu_residual(x, pre_g, w_in_ref, w_out_ref, post_g):
    h = _rms(x, pre_g).astype(jnp.bfloat16)
    acc = None
    for lo, hi in FFN_CHUNKS:
        g = _dot(h, w_in_ref[:, lo:hi])
        u = _dot(h, w_in_ref[:, D_FF + lo:D_FF + hi])
        a = (g * jax.nn.sigmoid(g) * u).astype(jnp.bfloat16)
        part = _dot(a, w_out_ref[lo:hi, :])
        acc = part if acc is None else acc + part
    return x + 0.5 * _rms(acc, post_g)


def _ffn_kernel(x_ref, pre_g_ref, w_in_ref, w_out_ref, post_g_ref, o_ref):
    o_ref[0] = _swiglu_residual(x_ref[0], pre_g_ref[...], w_in_ref, w_out_ref,
                                post_g_ref[...])


def _ffn(x, pre_g, w_in, w_out, post_g, tm):
    b, s, d = x.shape
    row = pl.BlockSpec((1, tm, d), lambda i, t: (i, t, 0))
    return pl.pallas_call(
        _ffn_kernel,
        grid=(b, s // tm),
        in_specs=[row, _const_spec(pre_g.shape), _const_spec(w_in.shape),
                  _const_spec(w_out.shape), _const_spec(post_g.shape)],
        out_specs=row,
        out_shape=jax.ShapeDtypeStruct(x.shape, jnp.float32),
        compiler_params=_params(2),
        name="ffn",
    )(x, pre_g, w_in, w_out, post_g)


def _proj_kernel(x_ref, ck_ref, sk_ref, cw_ref, sw_ref, pre_g_ref, w_ref, wvst_ref, qn_g_ref, wq_ref,
                 kvn_g_ref, wk_ref, wvt_ref, qa_ref, ka_ref, vt_ref, qw_ref, kw_ref, vw_ref):
    bf16 = jnp.bfloat16
    h = _rms(x_ref[0], pre_g_ref[...]).astype(bf16)
    ck_t, sk_t, cw_t, sw_t = ck_ref[...], sk_ref[...], cw_ref[...], sw_ref[...]
    lane = lax.broadcasted_iota(jnp.int32, (1, LANES), 1)
    low_half = lane < SWA_HEAD_DIM

    proj = _dot(h, w_ref[...])

    def cols(rng):
        return proj[:, rng[0]:rng[1]]

    qn = _rms(cols(_C_Q), qn_g_ref[...]).astype(bf16)
    qq = _dot(qn, wq_ref[...])
    n_q = MLA_HEADS * LANES
    for hd in range(MLA_HEADS):
        sl = slice(hd * LANES, (hd + 1) * LANES)
        q_rope = qq[:, sl] * ck_t + qq[:, n_q + hd * LANES:n_q + (hd + 1) * LANES] * sk_t
        qa_ref[0, hd] = (q_rope * (MLA_SCALE * LOG2_E)).astype(bf16)

    kvn = _rms(cols(_C_KV), kvn_g_ref[...]).astype(bf16)
    kk = _dot(kvn, wk_ref[...])
    k_rope = cols(_C_KR) * ck_t + cols(_C_KR_ROT) * sk_t
    for hd in range(MLA_HEADS):
        ka_ref[0, hd] = (kk[:, hd * LANES:(hd + 1) * LANES] + k_rope).astype(bf16)

    vt = _dot_nt(wvt_ref[...], kvn)
    is_v_row = lax.broadcasted_iota(jnp.int32, (LANES, 1), 0) < MLA_V
    for hd in range(MLA_HEADS):
        vt_ref[0, hd] = jnp.where(is_v_row, vt[hd * LANES:(hd + 1) * LANES], 1.0).astype(bf16)

    qs = cols(_C_QS)
    qs_rot = cols(_C_QS_ROT)
    for pair in range(SWA_HEADS // 2):
        sl = slice(pair * LANES, (pair + 1) * LANES)
        q_pair = (qs[:, sl] * cw_t + qs_rot[:, sl] * sw_t) * (SWA_SCALE * LOG2_E)
        qw_ref[0, 2 * pair] = jnp.where(low_half, q_pair, 0.0).astype(bf16)
        qw_ref[0, 2 * pair + 1] = jnp.where(low_half, 0.0, q_pair).astype(bf16)

    k_pair = cols(_C_KS) * cw_t + cols(_C_KS_ROT) * sw_t
    k_swapped = pltpu.roll(k_pair, SWA_HEAD_DIM, 1)
    kw_ref[0, 0] = jnp.where(low_half, k_pair, k_swapped).astype(bf16)
    kw_ref[0, 1] = jnp.where(low_half, k_swapped, k_pair).astype(bf16)

    vtw = _dot_nt(wvst_ref[...], h)
    ones = jnp.ones((SWA_HEAD_DIM, vtw.shape[1]), jnp.float32)
    for g in range(SWA_KV_HEADS):
        v_g = vtw[g * SWA_HEAD_DIM:(g + 1) * SWA_HEAD_DIM]
        vw_ref[0, g] = jnp.concatenate([v_g, ones], axis=0).astype(bf16)


def _proj(x1, tabs, pre_g, w_proj, w_vst, qn_g, wq, kvn_g, wk, wvt, tm):
    b, s, d = x1.shape
    row = pl.BlockSpec((1, tm, d), lambda i, t: (i, t, 0))
    tab_spec = pl.BlockSpec((tm, LANES), lambda i, t: (t, 0))

    def heads(n):
        return (pl.BlockSpec((1, n, tm, LANES), lambda i, t: (i, 0, t, 0)),
                jax.ShapeDtypeStruct((b, n, s, LANES), jnp.bfloat16))

    def heads_t(n):
        return (pl.BlockSpec((1, n, LANES, tm), lambda i, t: (i, 0, 0, t)),
                jax.ShapeDtypeStruct((b, n, LANES, s), jnp.bfloat16))

    outs = (heads(MLA_HEADS), heads(MLA_HEADS), heads_t(MLA_HEADS),
            heads(SWA_HEADS), heads(SWA_KV_HEADS), heads_t(SWA_KV_HEADS))
    consts = (pre_g, w_proj, w_vst, qn_g, wq, kvn_g, wk, wvt)
    return pl.pallas_call(
        _proj_kernel,
        grid=(b, s // tm),
        in_specs=[row] + [tab_spec] * len(tabs) + [_const_spec(c.shape) for c in consts],
        out_specs=[o[0] for o in outs],
        out_shape=[o[1] for o in outs],
        compiler_params=_params(2),
        name="proj",
    )(x1, *tabs, *consts)


LOG2_E = 1.4426950408889634


def _mla_kernel(q_ref, k_ref, vt_ref, kl_ref, vtl_ref, o_ref, s_a, s_b, ot_scr, *, tq, ck):
    seq = k_ref.shape[2]
    n_chunks = seq // ck
    n_tiles = seq // tq
    sublanes = 8
    lead_ok = lax.broadcasted_iota(jnp.int32, (BLOCK, 1), 0) >= LEAD_PAD

    def fold_max(s):
        return jnp.max(s.reshape(s.shape[0] // sublanes, sublanes, tq), axis=0)

    def rows(t):
        return pl.ds(pl.multiple_of(t * tq, tq), tq)

    def score_chunk(j, q, c, s_new, mx):
        s = _dot_nt(k_ref[0, j, c * ck:(c + 1) * ck, :], q)
        s_new[c] = s
        return jnp.maximum(mx, fold_max(s))

    def value_chunk(j, c, s_old, m, acc):
        p = jnp.exp2(s_old[c] - m).astype(jnp.bfloat16)
        return acc + _dot(vt_ref[0, j, :, c * ck:(c + 1) * ck], p)

    def emit(j, t, acc):
        o_t = acc[:MLA_V] * (1.0 / acc[MLA_V:MLA_V + 1])
        if j % 2 == 0:
            ot_scr[t] = o_t
        else:
            both = jnp.concatenate([ot_scr[t], o_t], axis=0)
            slab = slice((j // 2) * LANES, (j // 2 + 1) * LANES)
            o_ref[0, rows(t), slab] = both.T.astype(o_ref.dtype)

    def stage(new, old, s_new, s_old, m_prev, acc_prev):
        j, t = new
        q = q_ref[0, j, rows(t), :]
        s_lead = jnp.where(lead_ok, _dot_nt(kl_ref[0, j], q), NEG)
        mx, acc = fold_max(s_lead), acc_prev
        for c in range(n_chunks):
            mx = score_chunk(j, q, c, s_new, mx)
            if old is not None:
                acc = value_chunk(old[0], c, s_old, m_prev, acc)
        if old is not None:
            emit(*old, acc)
        m = jnp.max(mx, axis=0, keepdims=True)
        return m, _dot(vtl_ref[0, j], jnp.exp2(s_lead - m).astype(jnp.bfloat16))

    carry = stage((0, 0), None, s_a, None, None, None)
    n_heads = q_ref.shape[1]
    for j in range(n_heads):

        def pair(i, carry, j=j):
            m, acc = stage((j, 2 * i + 1), (j, 2 * i), s_b, s_a, *carry)
            return stage((j, 2 * i + 2), (j, 2 * i + 1), s_a, s_b, m, acc)

        carry = lax.fori_loop(0, n_tiles // 2 - 1, pair, carry)
        carry = stage((j, n_tiles - 1), (j, n_tiles - 2), s_b, s_a, *carry)
        if j + 1 < n_heads:
            carry = stage((j + 1, 0), (j, n_tiles - 1), s_a, s_b, *carry)
    m, acc = carry
    last = n_heads - 1
    for c in range(n_chunks):
        acc = value_chunk(last, c, s_b, m, acc)
    emit(last, n_tiles - 1, acc)


def _mla_heads_per_step(nh, s, tq, ck):
    bf16_bytes, f32_bytes, pipeline_buffers = 2, 4, 2
    scratch = 2 * s * tq * f32_bytes + s * MLA_V * f32_bytes
    in_flight = 2 * ck * tq * f32_bytes
    for hp in range(nh, 0, -2):
        per_head = (3 * s * LANES + s * MLA_V) * bf16_bytes
        if nh % hp == 0 and pipeline_buffers * hp * per_head + scratch + in_flight <= VMEM_LIMIT:
            return hp
    raise ValueError("sequence too long for the resident-key MLA kernel")


def _mla(q, k, vt, k_lead, vt_lead, tq, ck):
    b, nh, s, _ = q.shape
    hp = _mla_heads_per_step(nh, s, tq, ck)
    assert s % ck == 0 and s % (2 * tq) == 0
    seq_spec = pl.BlockSpec((1, hp, s, LANES), lambda i, p: (i, p, 0, 0))
    vt_spec = pl.BlockSpec((1, hp, LANES, s), lambda i, p: (i, p, 0, 0))
    lead_spec = pl.BlockSpec((1, hp, BLOCK, LANES), lambda i, p: (0, p, 0, 0))
    score_buf = pltpu.VMEM((s // ck, ck, tq), jnp.float32)
    return pl.pallas_call(
        functools.partial(_mla_kernel, tq=tq, ck=ck),
        grid=(b, nh // hp),
        in_specs=[seq_spec, seq_spec, vt_spec, lead_spec, lead_spec],
        out_specs=pl.BlockSpec((1, s, hp * MLA_V), lambda i, p: (i, 0, p)),
        out_shape=jax.ShapeDtypeStruct((b, s, nh * MLA_V), jnp.bfloat16),
        scratch_shapes=[score_buf, score_buf,
                        pltpu.VMEM((s // tq, MLA_V, tq), jnp.float32)],
        compiler_params=_params(2),
        name="mla",
    )(q, k, vt, k_lead, vt_lead)


def _window_kernel(sink_ref, q_ref, kl_ref, kc_ref, kr_ref, km_ref,
                   vl_ref, vc_ref, vr_ref, vm_ref, o_ref):
    tq = q_ref.shape[2]
    n_blk = tq // BLOCK
    t = pl.program_id(1)
    n_t = pl.num_programs(1)
    n_keys = 4 * BLOCK
    n_cols = SWA_GROUP * BLOCK
    key = lax.broadcasted_iota(jnp.int32, (n_keys, n_cols), 0)
    query = lax.broadcasted_iota(jnp.int32, (n_keys, n_cols), 1) % BLOCK
    in_left = jnp.logical_and(key < BLOCK, key >= query)
    in_right = jnp.logical_and(jnp.logical_and(key >= 2 * BLOCK, key < 3 * BLOCK),
                               key - 2 * BLOCK <= query)
    in_rest = jnp.logical_or(jnp.logical_and(key >= BLOCK, key < 2 * BLOCK),
                             key >= 3 * BLOCK + LEAD_PAD)
    sublanes = 8

    def keys_of(blk, left_ref, center_ref, right_ref, lead_ref, g):
        first, last = blk == 0, blk == n_blk - 1
        return jnp.concatenate(
            [left_ref[0, g] if first else center_ref[0, g, (blk - 1) * BLOCK:blk * BLOCK, :],
             center_ref[0, g, blk * BLOCK:(blk + 1) * BLOCK, :],
             right_ref[0, g] if last else center_ref[0, g, (blk + 1) * BLOCK:(blk + 2) * BLOCK, :],
             lead_ref[0, g]], axis=0)

    def values_t_of(blk, g):
        first, last = blk == 0, blk == n_blk - 1
        return jnp.concatenate(
            [vl_ref[0, g] if first else vc_ref[0, g, :, (blk - 1) * BLOCK:blk * BLOCK],
             vc_ref[0, g, :, blk * BLOCK:(blk + 1) * BLOCK],
             vr_ref[0, g] if last else vc_ref[0, g, :, (blk + 1) * BLOCK:(blk + 2) * BLOCK],
             vm_ref[0, g]], axis=1)

    def scores(g, blk):
        ok_left = jnp.logical_and(in_left, t > 0) if blk == 0 else in_left
        ok_right = jnp.logical_and(in_right, t < n_t - 1) if blk == n_blk - 1 else in_right
        ok = jnp.logical_or(jnp.logical_or(ok_left, ok_right), in_rest)
        q = q_ref[0, g * SWA_GROUP:(g + 1) * SWA_GROUP, blk * BLOCK:(blk + 1) * BLOCK, :]
        s = _dot_nt(keys_of(blk, kl_ref, kc_ref, kr_ref, km_ref, g),
                    q.reshape(n_cols, LANES))
        return jnp.where(ok, s, NEG)

    def finish(g, blk, s):
        sink = jnp.concatenate(
            [jnp.full((1, BLOCK), sink_ref[g * SWA_GROUP + r] * LOG2_E, jnp.float32)
             for r in range(SWA_GROUP)], axis=1)
        folded = jnp.max(s.reshape(n_keys // sublanes, sublanes, n_cols), axis=0)
        m = jnp.maximum(jnp.max(folded, axis=0, keepdims=True), sink)
        p = jnp.exp2(s - m).astype(jnp.bfloat16)
        acc = _dot(values_t_of(blk, g), p)
        o_t = acc[:SWA_HEAD_DIM] * (
            1.0 / (acc[SWA_HEAD_DIM:SWA_HEAD_DIM + 1] + jnp.exp2(sink - m)))
        for pair in range(SWA_GROUP // 2):
            both = jnp.concatenate(
                [o_t[:, (2 * pair) * BLOCK:(2 * pair + 1) * BLOCK],
                 o_t[:, (2 * pair + 1) * BLOCK:(2 * pair + 2) * BLOCK]], axis=0)
            slab = g * (SWA_GROUP // 2) + pair
            o_ref[0, blk * BLOCK:(blk + 1) * BLOCK, slab * LANES:(slab + 1) * LANES] = (
                both.T.astype(o_ref.dtype))

    units = [(g, blk) for g in range(SWA_KV_HEADS) for blk in range(n_blk)]
    s_next = scores(*units[0])
    for i, unit in enumerate(units):
        s_cur = s_next
        if i + 1 < len(units):
            s_next = scores(*units[i + 1])
        finish(*unit, s_cur)


def _window(sink, q, k, v, k_lead, v_lead, tq):
    b, nh, s, _ = q.shape
    n_blk = s // BLOCK
    per = tq // BLOCK

    def specs(n, transposed):
        def spec(rows, index):
            if transposed:
                return pl.BlockSpec((1, n, LANES, rows), lambda i, t: (*index(i, t)[:2], 0,
                                                                       index(i, t)[2]))
            return pl.BlockSpec((1, n, rows, LANES), lambda i, t: (*index(i, t), 0))

        return [spec(BLOCK, lambda i, t: (i, 0, jnp.maximum(t * per - 1, 0))),
                spec(tq, lambda i, t: (i, 0, t)),
                spec(BLOCK, lambda i, t: (i, 0, jnp.minimum((t + 1) * per, n_blk - 1))),
                spec(BLOCK, lambda i, t: (0, 0, 0))]

    smem = pl.BlockSpec(memory_space=pltpu.SMEM)
    return pl.pallas_call(
        _window_kernel,
        grid=(b, s // tq),
        in_specs=([smem, specs(nh, False)[1]] + specs(SWA_KV_HEADS, False)
                  + specs(SWA_KV_HEADS, True)),
        out_specs=pl.BlockSpec((1, tq, nh * SWA_HEAD_DIM), lambda i, t: (i, t, 0)),
        out_shape=jax.ShapeDtypeStruct((b, s, nh * SWA_HEAD_DIM), jnp.bfloat16),
        compiler_params=_params(2),
        name="window",
    )(sink, q, k, k, k, k_lead, v, v, v, v_lead)


def _mixout_kernel(x_ref, oa_ref, ob_ref, pre_g_ref, wg_ref, woa_ref, wob_ref, wout_ref,
                   post_g_ref, o_ref):
    x1 = x_ref[0]
    h = _rms(x1, pre_g_ref[...]).astype(jnp.bfloat16)
    y_a = _dot(oa_ref[0], woa_ref[...])
    y_b = _dot(ob_ref[0], wob_ref[...])
    merged = (jax.nn.sigmoid(_dot(h, wg_ref[:, :D_MODEL])) * y_a
              + jax.nn.sigmoid(_dot(h, wg_ref[:, D_MODEL:])) * y_b)
    o_ref[0] = x1 + _rms(_dot(merged.astype(jnp.bfloat16), wout_ref[...]), post_g_ref[...])


def _mixout(x1, o_a, o_b, pre_g, wg, woa, wob, wout, post_g, tm):
    b, s, d = x1.shape
    row = pl.BlockSpec((1, tm, d), lambda i, t: (i, t, 0))
    half = pl.BlockSpec((1, tm, o_a.shape[2]), lambda i, t: (i, t, 0))
    return pl.pallas_call(
        _mixout_kernel,
        grid=(b, s // tm),
        in_specs=[row, half, half, _const_spec(pre_g.shape), _const_spec(wg.shape),
                  _const_spec(woa.shape), _const_spec(wob.shape), _const_spec(wout.shape),
                  _const_spec(post_g.shape)],
        out_specs=row,
        out_shape=jax.ShapeDtypeStruct(x1.shape, jnp.float32),
        compiler_params=_params(2),
        name="mixout",
    )(x1, o_a, o_b, pre_g, wg, woa, wob, wout, post_g)


def _rot_cols(w, n_heads, dim):
    w = w.reshape(w.shape[0], n_heads, dim)
    half = dim // 2
    return jnp.concatenate([-w[..., half:], w[..., :half]], axis=-1).reshape(w.shape[0], -1)


def _pack_weights(w_in, w_uq, w_ukv):
    bf16 = jnp.bfloat16
    d = w_in.shape[0]
    o = 0
    parts = []
    for n in (MLA_Q_LORA, MLA_KV_LORA, MLA_ROPE, SWA_HEADS * SWA_HEAD_DIM,
              SWA_KV_HEADS * SWA_HEAD_DIM, SWA_KV_HEADS * SWA_HEAD_DIM, D_MODEL, D_MODEL):
        parts.append(w_in[:, o:o + n])
        o += n
    w_cq, w_ckv, w_kr, w_qs, w_ks, w_vs, w_ga, w_gb = parts

    def rope_slab(w):
        return jnp.concatenate([jnp.zeros((d, MLA_NOPE), w.dtype), w,
                                jnp.zeros((d, LANES - MLA_NOPE - MLA_ROPE), w.dtype)], axis=1)

    w_proj = jnp.concatenate(
        [w_cq, w_ckv, w_qs, _rot_cols(w_qs, SWA_HEADS, SWA_HEAD_DIM),
         w_ks, _rot_cols(w_ks, SWA_KV_HEADS, SWA_HEAD_DIM),
         rope_slab(w_kr), rope_slab(_rot_cols(w_kr, 1, MLA_ROPE))], axis=1).astype(bf16)
    w_gate = jnp.concatenate([w_ga, w_gb], axis=1).astype(bf16)
    w_vst = w_vs.T.astype(bf16)

    r = w_uq.shape[0]
    uq = w_uq.reshape(r, MLA_HEADS, MLA_NOPE + MLA_ROPE)
    pad = jnp.zeros((r, MLA_HEADS, LANES - MLA_NOPE - MLA_ROPE), w_uq.dtype)
    q_plain = jnp.concatenate([uq, pad], axis=-1)
    q_rot = jnp.concatenate(
        [jnp.zeros((r, MLA_HEADS, MLA_NOPE), w_uq.dtype),
         _rot_cols(uq[..., MLA_NOPE:].reshape(r, -1), MLA_HEADS, MLA_ROPE).reshape(
             r, MLA_HEADS, MLA_ROPE), pad], axis=-1)
    wq = jnp.concatenate([q_plain.reshape(r, -1), q_rot.reshape(r, -1)], axis=1).astype(bf16)

    r = w_ukv.shape[0]
    ukv = w_ukv.reshape(r, MLA_HEADS, MLA_NOPE + MLA_V)
    zeros = jnp.zeros((r, MLA_HEADS, LANES - MLA_NOPE), w_ukv.dtype)
    k_slab = jnp.concatenate([ukv[..., :MLA_NOPE], zeros], axis=-1)
    wk = k_slab.reshape(r, -1).astype(bf16)
    v_slab = jnp.concatenate([ukv[..., MLA_NOPE:], zeros], axis=-1)
    wvt = v_slab.reshape(r, -1).T.astype(bf16)
    return w_proj, w_gate, wq, wk, wvt, w_vst


def _rope_tables(pos):
    pos = pos.astype(jnp.float32)[:, None]
    lane = jnp.arange(LANES)

    def inv_freq(half):
        return ROPE_THETA ** (-jnp.arange(half, dtype=jnp.float32) / half)

    half_k = MLA_ROPE // 2
    ang_k = pos * inv_freq(half_k)[(lane - MLA_NOPE) % half_k][None, :]
    is_rope = jnp.logical_and(lane >= MLA_NOPE, lane < MLA_NOPE + MLA_ROPE)[None, :]
    ck = jnp.where((lane < MLA_NOPE)[None, :], 1.0, jnp.where(is_rope, jnp.cos(ang_k), 0.0))
    sk = jnp.where(is_rope, jnp.sin(ang_k), 0.0)
    half_w = SWA_HEAD_DIM // 2
    ang_w = pos * inv_freq(half_w)[lane % half_w][None, :]
    return ck, sk, jnp.cos(ang_w), jnp.sin(ang_w)


def _row_tile(s):
    return min(s, 512)


def kernel(x_prompt, x_sample, meta_tokens, ffn1_pre_g, ffn1_w_in, ffn1_w_out, ffn1_post_g,
           mix_pre_g, w_in, q_norm_g, w_uq, kv_norm_g, w_ukv, sink, w_o_a, w_o_b, w_out,
           mix_post_g, ffn2_pre_g, ffn2_w_in, ffn2_w_out, ffn2_post_g):
    assert ffn1_w_in.shape[0] == 1, "single layer"
    bf16 = jnp.bfloat16
    f1 = (ffn1_pre_g, ffn1_w_in[0].astype(bf16), ffn1_w_out[0].astype(bf16), ffn1_post_g)
    f2 = (ffn2_pre_g, ffn2_w_in[0].astype(bf16), ffn2_w_out[0].astype(bf16), ffn2_post_g)
    w_proj, w_gate, wq, wk, wvt, w_vst = _pack_weights(w_in[0], w_uq[0], w_ukv[0])
    woa, wob, wout = w_o_a[0].astype(bf16), w_o_b[0].astype(bf16), w_out[0].astype(bf16)
    sink = sink[0]

    def front(x, tabs):
        tm = _row_tile(x.shape[1])
        x1 = _ffn(x, *f1, tm)
        return x1, _proj(x1, tabs, mix_pre_g, w_proj, w_vst, q_norm_g, wq, kv_norm_g, wk, wvt,
                         tm)

    lead = jnp.concatenate([jnp.zeros((LEAD_PAD, D_MODEL), x_prompt.dtype),
                            meta_tokens.astype(x_prompt.dtype)], axis=0)[None]
    _, (_, ka_lead, va_lead, _, kw_lead, vw_lead) = front(
        lead, _rope_tables(jnp.arange(BLOCK) - LEAD_PAD))
    seq_tabs = _rope_tables(jnp.arange(max(x_prompt.shape[1], x_sample.shape[1])) + N_META)

    def trunk(x):
        s = x.shape[1]
        tm = _row_tile(s)
        x1, (qa, ka, va, qw, kw, vw) = front(x, seq_tabs)
        o_a = _mla(qa, ka, va, ka_lead, va_lead, tq=MXU_COLS, ck=2048)
        o_b = _window(sink, qw, kw, vw, kw_lead, vw_lead, tq=4 * BLOCK)
        x2 = _mixout(x1, o_a, o_b, mix_pre_g, w_gate, woa, wob, wout, mix_post_g, tm)
        return _ffn(x2, *f2, tm)

    return (trunk(x_prompt), trunk(x_sample))
```

```python
import functools

import jax
import jax.numpy as jnp
from jax import lax
from jax.experimental import pallas as pl
from jax.experimental.pallas import tpu as pltpu

D_MODEL = 1024
N_META = 16
BLOCK = 128
WINDOW = 128
LEAD_PAD = BLOCK - N_META
ROPE_THETA = 10000.0
EPS = 1e-6
NEG = -1e30
MLA_HEADS = 8
MLA_Q_LORA = 384
MLA_KV_LORA = 256
MLA_NOPE = 64
MLA_ROPE = 32
MLA_V = 64
SWA_HEADS = 8
SWA_KV_HEADS = 2
SWA_HEAD_DIM = 64
SWA_GROUP = SWA_HEADS // SWA_KV_HEADS
D_FF = 2816

LANES = 128
ROPE_PARTNER = LANES // 2
MLA_SCALE = (MLA_NOPE + MLA_ROPE) ** -0.5
SWA_SCALE = SWA_HEAD_DIM ** -0.5
V7X_VMEM_BYTES = 64 * 1024 * 1024
VMEM_LIMIT = (V7X_VMEM_BYTES * 7) // 8

_C_Q = (0, MLA_Q_LORA)
_C_KV = (_C_Q[1], _C_Q[1] + MLA_KV_LORA)
_C_QS = (_C_KV[1], _C_KV[1] + SWA_HEADS * SWA_HEAD_DIM)
_C_KS = (_C_QS[1], _C_QS[1] + LANES)
_C_KR = (_C_KS[1], _C_KS[1] + LANES)
D_PROJ = _C_KR[1]

MXU_COLS = 256
FFN_CHUNKS = ((0, 6 * MXU_COLS), (6 * MXU_COLS, D_FF))
assert all((hi - lo) % MXU_COLS == 0 for lo, hi in FFN_CHUNKS)


def _const_spec(shape):
    zeros = (0,) * len(shape)
    return pl.BlockSpec(shape, lambda *_: zeros, pipeline_mode=pl.Buffered(1))


def _params(n_axes):
    return pltpu.CompilerParams(
        dimension_semantics=("parallel",) * n_axes, vmem_limit_bytes=VMEM_LIMIT)


def _rms(x, g):
    y = x * lax.rsqrt(jnp.mean(x * x, axis=-1, keepdims=True) + EPS)
    return y * g


def _dot(a, b):
    return jnp.dot(a, b, preferred_element_type=jnp.float32)


def _dot_nt(a, b):
    return lax.dot_general(a, b, (((1,), (1,)), ((), ())),
                           preferred_element_type=jnp.float32)


def _swiglu_residual(x, pre_g, w_in_ref, w_out_ref, post_g):
    h = _rms(x, pre_g).astype(jnp.bfloat16)
    acc = None
    for lo, hi in FFN_CHUNKS:
        g = _dot(h, w_in_ref[:, lo:hi])
        u = _dot(h, w_in_ref[:, D_FF + lo:D_FF + hi])
        a = (g * jax.nn.sigmoid(g) * u).astype(jnp.bfloat16)
        part = _dot(a, w_out_ref[lo:hi, :])
        acc = part if acc is None else acc + part
    return x + 0.5 * _rms(acc, post_g)


def _ffn_kernel(x_ref, pre_g_ref, w_in_ref, w_out_ref, post_g_ref, o_ref):
    o_ref[0] = _swiglu_residual(x_ref[0], pre_g_ref[...], w_in_ref, w_out_ref,
                                post_g_ref[...])


def _ffn(x, pre_g, w_in, w_out, post_g, tm):
    b, s, d = x.shape
    row = pl.BlockSpec((1, tm, d), lambda i, t: (i, t, 0))
    return pl.pallas_call(
        _ffn_kernel,
        grid=(b, s // tm),
        in_specs=[row, _const_spec(pre_g.shape), _const_spec(w_in.shape),
                  _const_spec(w_out.shape), _const_spec(post_g.shape)],
        out_specs=row,
        out_shape=jax.ShapeDtypeStruct(x.shape, jnp.float32),
        compiler_params=_params(2),
        name="ffn",
    )(x, pre_g, w_in, w_out, post_g)


def _proj_kernel(x_ref, ck_ref, sk_ref, cw_ref, sw_ref, pre_g_ref, w_ref, wvst_ref, qn_g_ref, wq_ref,
                 kvn_g_ref, wk_ref, wvt_ref, qa_ref, ka_ref, vt_ref, qw_ref, kw_ref, vw_ref):
    bf16 = jnp.bfloat16
    h = _rms(x_ref[0], pre_g_ref[...]).astype(bf16)
    ck_t, sk_t, cw_t, sw_t = ck_ref[...], sk_ref[...], cw_ref[...], sw_ref[...]
    lane = lax.broadcasted_iota(jnp.int32, (1, LANES), 1)

    proj = _dot(h, w_ref[...])

    def cols(rng):
        return proj[:, rng[0]:rng[1]]

    def rope(x, cos, sin_signed):
        return x * cos + pltpu.roll(x, ROPE_PARTNER, 1) * sin_signed

    qn = _rms(cols(_C_Q), qn_g_ref[...]).astype(bf16)
    qq = _dot(qn, wq_ref[...])
    for hd in range(MLA_HEADS):
        q_rope = rope(qq[:, hd * LANES:(hd + 1) * LANES], ck_t, sk_t)
        qa_ref[0, hd] = (q_rope * (MLA_SCALE * LOG2_E)).astype(bf16)

    kvn = _rms(cols(_C_KV), kvn_g_ref[...]).astype(bf16)
    kk = _dot(kvn, wk_ref[...])
    k_rope = rope(cols(_C_KR), ck_t, sk_t)
    for hd in range(MLA_HEADS):
        ka_ref[0, hd] = (kk[:, hd * LANES:(hd + 1) * LANES] + k_rope).astype(bf16)

    vt = _dot_nt(wvt_ref[...], kvn)
    is_v_row = lax.broadcasted_iota(jnp.int32, (LANES, 1), 0) < MLA_V
    for hd in range(MLA_HEADS):
        vt_ref[0, hd] = jnp.where(is_v_row, vt[hd * LANES:(hd + 1) * LANES], 1.0).astype(bf16)

    quarter = SWA_HEAD_DIM // 2
    even_lanes = lane % SWA_HEAD_DIM < quarter
    qs = cols(_C_QS)
    for pair in range(SWA_HEADS // 2):
        q_pair = rope(qs[:, pair * LANES:(pair + 1) * LANES], cw_t, sw_t) * (SWA_SCALE * LOG2_E)
        qw_ref[0, 2 * pair] = jnp.where(even_lanes, q_pair, 0.0).astype(bf16)
        qw_ref[0, 2 * pair + 1] = jnp.where(even_lanes, 0.0, q_pair).astype(bf16)

    k_pair = rope(cols(_C_KS), cw_t, sw_t)
    from_below = pltpu.roll(k_pair, quarter, 1)
    from_above = pltpu.roll(k_pair, LANES - quarter, 1)
    kw_ref[0, 0] = jnp.where(even_lanes, k_pair, from_below).astype(bf16)
    kw_ref[0, 1] = jnp.where(even_lanes, from_above, k_pair).astype(bf16)

    vtw = _dot_nt(wvst_ref[...], h)
    ones = jnp.ones((SWA_HEAD_DIM, vtw.shape[1]), jnp.float32)
    for g in range(SWA_KV_HEADS):
        v_g = vtw[g * SWA_HEAD_DIM:(g + 1) * SWA_HEAD_DIM]
        vw_ref[0, g] = jnp.concatenate([v_g, ones], axis=0).astype(bf16)


def _proj(x1, tabs, pre_g, w_proj, w_vst, qn_g, wq, kvn_g, wk, wvt, tm):
    b, s, d = x1.shape
    row = pl.BlockSpec((1, tm, d), lambda i, t: (i, t, 0))
    tab_spec = pl.BlockSpec((tm, LANES), lambda i, t: (t, 0))

    def heads(n):
        return (pl.BlockSpec((1, n, tm, LANES), lambda i, t: (i, 0, t, 0)),
                jax.ShapeDtypeStruct((b, n, s, LANES), jnp.bfloat16))

    def heads_t(n):
        return (pl.BlockSpec((1, n, LANES, tm), lambda i, t: (i, 0, 0, t)),
                jax.ShapeDtypeStruct((b, n, LANES, s), jnp.bfloat16))

    outs = (heads(MLA_HEADS), heads(MLA_HEADS), heads_t(MLA_HEADS),
            heads(SWA_HEADS), heads(SWA_KV_HEADS), heads_t(SWA_KV_HEADS))
    consts = (pre_g, w_proj, w_vst, qn_g, wq, kvn_g, wk, wvt)
    return pl.pallas_call(
        _proj_kernel,
        grid=(b, s // tm),
        in_specs=[row] + [tab_spec] * len(tabs) + [_const_spec(c.shape) for c in consts],
        out_specs=[o[0] for o in outs],
        out_shape=[o[1] for o in outs],
        compiler_params=_params(2),
        name="proj",
    )(x1, *tabs, *consts)


LOG2_E = 1.4426950408889634


def _mla_kernel(q_ref, k_ref, vt_ref, kl_ref, vtl_ref, o_ref, s_a, s_b, ot_scr, *, tq, ck):
    seq = k_ref.shape[2]
    n_chunks = seq // ck
    n_tiles = seq // tq
    sublanes = 8
    lead_ok = lax.broadcasted_iota(jnp.int32, (BLOCK, 1), 0) >= LEAD_PAD

    def fold_max(s):
        return jnp.max(s.reshape(s.shape[0] // sublanes, sublanes, tq), axis=0)

    def rows(t):
        return pl.ds(pl.multiple_of(t * tq, tq), tq)

    def score_chunk(j, q, c, s_new, mx):
        s = _dot_nt(k_ref[0, j, c * ck:(c + 1) * ck, :], q)
        s_new[c] = s
        return jnp.maximum(mx, fold_max(s))

    def value_chunk(j, c, s_old, m, acc):
        p = jnp.exp2(s_old[c] - m).astype(jnp.bfloat16)
        return acc + _dot(vt_ref[0, j, :, c * ck:(c + 1) * ck], p)

    def emit(j, t, acc):
        o_t = acc[:MLA_V] * (1.0 / acc[MLA_V:MLA_V + 1])
        if j % 2 == 0:
            ot_scr[t] = o_t
        else:
            both = jnp.concatenate([ot_scr[t], o_t], axis=0)
            slab = slice((j // 2) * LANES, (j // 2 + 1) * LANES)
            o_ref[0, rows(t), slab] = both.T.astype(o_ref.dtype)

    def stage(new, old, s_new, s_old, m_prev, acc_prev):
        j, t = new
        q = q_ref[0, j, rows(t), :]
        s_lead = jnp.where(lead_ok, _dot_nt(kl_ref[0, j], q), NEG)
        mx, acc = fold_max(s_lead), acc_prev
        for c in range(n_chunks):
            mx = score_chunk(j, q, c, s_new, mx)
            if old is not None:
                acc = value_chunk(old[0], c, s_old, m_prev, acc)
        if old is not None:
            emit(*old, acc)
        m = jnp.max(mx, axis=0, keepdims=True)
        return m, _dot(vtl_ref[0, j], jnp.exp2(s_lead - m).astype(jnp.bfloat16))

    carry = stage((0, 0), None, s_a, None, None, None)
    n_heads = q_ref.shape[1]
    for j in range(n_heads):

        def pair(i, carry, j=j):
            m, acc = stage((j, 2 * i + 1), (j, 2 * i), s_b, s_a, *carry)
            return stage((j, 2 * i + 2), (j, 2 * i + 1), s_a, s_b, m, acc)

        carry = lax.fori_loop(0, n_tiles // 2 - 1, pair, carry)
        carry = stage((j, n_tiles - 1), (j, n_tiles - 2), s_b, s_a, *carry)
        if j + 1 < n_heads:
            carry = stage((j + 1, 0), (j, n_tiles - 1), s_a, s_b, *carry)
    m, acc = carry
    last = n_heads - 1
    for c in range(n_chunks):
        acc = value_chunk(last, c, s_b, m, acc)
    emit(last, n_tiles - 1, acc)


def _mla_heads_per_step(nh, s, tq, ck):
    bf16_bytes, f32_bytes, pipeline_buffers = 2, 4, 2
    scratch = 2 * s * tq * f32_bytes + s * MLA_V * f32_bytes
    in_flight = 2 * ck * tq * f32_bytes
    for hp in range(nh, 0, -2):
        per_head = (3 * s * LANES + s * MLA_V) * bf16_bytes
        if nh % hp == 0 and pipeline_buffers * hp * per_head + scratch + in_flight <= VMEM_LIMIT:
            return hp
    raise ValueError("sequence too long for the resident-key MLA kernel")


def _mla(q, k, vt, k_lead, vt_lead, tq, ck):
    b, nh, s, _ = q.shape
    hp = _mla_heads_per_step(nh, s, tq, ck)
    assert s % ck == 0 and s % (2 * tq) == 0
    seq_spec = pl.BlockSpec((1, hp, s, LANES), lambda i, p: (i, p, 0, 0))
    vt_spec = pl.BlockSpec((1, hp, LANES, s), lambda i, p: (i, p, 0, 0))
    lead_spec = pl.BlockSpec((1, hp, BLOCK, LANES), lambda i, p: (0, p, 0, 0))
    score_buf = pltpu.VMEM((s // ck, ck, tq), jnp.float32)
    return pl.pallas_call(
        functools.partial(_mla_kernel, tq=tq, ck=ck),
        grid=(b, nh // hp),
        in_specs=[seq_spec, seq_spec, vt_spec, lead_spec, lead_spec],
        out_specs=pl.BlockSpec((1, s, hp * MLA_V), lambda i, p: (i, 0, p)),
        out_shape=jax.ShapeDtypeStruct((b, s, nh * MLA_V), jnp.bfloat16),
        scratch_shapes=[score_buf, score_buf,
                        pltpu.VMEM((s // tq, MLA_V, tq), jnp.float32)],
        compiler_params=_params(2),
        name="mla",
    )(q, k, vt, k_lead, vt_lead)


def _window_kernel(sink_ref, q_ref, kl_ref, kc_ref, kr_ref, km_ref,
                   vl_ref, vc_ref, vr_ref, vm_ref, o_ref):
    tq = q_ref.shape[2]
    n_blk = tq // BLOCK
    t = pl.program_id(1)
    n_t = pl.num_programs(1)
    n_keys = 4 * BLOCK
    n_cols = SWA_GROUP * BLOCK
    key = lax.broadcasted_iota(jnp.int32, (n_keys, n_cols), 0)
    query = lax.broadcasted_iota(jnp.int32, (n_keys, n_cols), 1) % BLOCK
    in_left = jnp.logical_and(key < BLOCK, key >= query)
    in_right = jnp.logical_and(jnp.logical_and(key >= 2 * BLOCK, key < 3 * BLOCK),
                               key - 2 * BLOCK <= query)
    in_rest = jnp.logical_or(jnp.logical_and(key >= BLOCK, key < 2 * BLOCK),
                             key >= 3 * BLOCK + LEAD_PAD)
    sublanes = 8

    def keys_of(blk, left_ref, center_ref, right_ref, lead_ref, g):
        first, last = blk == 0, blk == n_blk - 1
        return jnp.concatenate(
            [left_ref[0, g] if first else center_ref[0, g, (blk - 1) * BLOCK:blk * BLOCK, :],
             center_ref[0, g, blk * BLOCK:(blk + 1) * BLOCK, :],
             right_ref[0, g] if last else center_ref[0, g, (blk + 1) * BLOCK:(blk + 2) * BLOCK, :],
             lead_ref[0, g]], axis=0)

    def values_t_of(blk, g):
        first, last = blk == 0, blk == n_blk - 1
        return jnp.concatenate(
            [vl_ref[0, g] if first else vc_ref[0, g, :, (blk - 1) * BLOCK:blk * BLOCK],
             vc_ref[0, g, :, blk * BLOCK:(blk + 1) * BLOCK],
             vr_ref[0, g] if last else vc_ref[0, g, :, (blk + 1) * BLOCK:(blk + 2) * BLOCK],
             vm_ref[0, g]], axis=1)

    def scores(g, blk):
        ok_left = jnp.logical_and(in_left, t > 0) if blk == 0 else in_left
        ok_right = jnp.logical_and(in_right, t < n_t - 1) if blk == n_blk - 1 else in_right
        ok = jnp.logical_or(jnp.logical_or(ok_left, ok_right), in_rest)
        q = q_ref[0, g * SWA_GROUP:(g + 1) * SWA_GROUP, blk * BLOCK:(blk + 1) * BLOCK, :]
        s = _dot_nt(keys_of(blk, kl_ref, kc_ref, kr_ref, km_ref, g),
                    q.reshape(n_cols, LANES))
        return jnp.where(ok, s, NEG)

    def finish(g, blk, s):
        sink = jnp.concatenate(
            [jnp.full((1, BLOCK), sink_ref[g * SWA_GROUP + r] * LOG2_E, jnp.float32)
             for r in range(SWA_GROUP)], axis=1)
        folded = jnp.max(s.reshape(n_keys // sublanes, sublanes, n_cols), axis=0)
        m = jnp.maximum(jnp.max(folded, axis=0, keepdims=True), sink)
        p = jnp.exp2(s - m).astype(jnp.bfloat16)
        acc = _dot(values_t_of(blk, g), p)
        o_t = acc[:SWA_HEAD_DIM] * (
            1.0 / (acc[SWA_HEAD_DIM:SWA_HEAD_DIM + 1] + jnp.exp2(sink - m)))
        for pair in range(SWA_GROUP // 2):
            both = jnp.concatenate(
                [o_t[:, (2 * pair) * BLOCK:(2 * pair + 1) * BLOCK],
                 o_t[:, (2 * pair + 1) * BLOCK:(2 * pair + 2) * BLOCK]], axis=0)
            slab = g * (SWA_GROUP // 2) + pair
            o_ref[0, blk * BLOCK:(blk + 1) * BLOCK, slab * LANES:(slab + 1) * LANES] = (
                both.T.astype(o_ref.dtype))

    units = [(g, blk) for g in range(SWA_KV_HEADS) for blk in range(n_blk)]
    s_next = scores(*units[0])
    for i, unit in enumerate(units):
        s_cur = s_next
        if i + 1 < len(units):
            s_next = scores(*units[i + 1])
        finish(*unit, s_cur)


def _window(sink, q, k, v, k_lead, v_lead, tq):
    b, nh, s, _ = q.shape
    n_blk = s // BLOCK
    per = tq // BLOCK

    def specs(n, transposed):
        def spec(rows, index):
            if transposed:
                return pl.BlockSpec((1, n, LANES, rows), lambda i, t: (*index(i, t)[:2], 0,
                                                                       index(i, t)[2]))
            return pl.BlockSpec((1, n, rows, LANES), lambda i, t: (*index(i, t), 0))

        return [spec(BLOCK, lambda i, t: (i, 0, jnp.maximum(t * per - 1, 0))),
                spec(tq, lambda i, t: (i, 0, t)),
                spec(BLOCK, lambda i, t: (i, 0, jnp.minimum((t + 1) * per, n_blk - 1))),
                spec(BLOCK, lambda i, t: (0, 0, 0))]

    smem = pl.BlockSpec(memory_space=pltpu.SMEM)
    return pl.pallas_call(
        _window_kernel,
        grid=(b, s // tq),
        in_specs=([smem, specs(nh, False)[1]] + specs(SWA_KV_HEADS, False)
                  + specs(SWA_KV_HEADS, True)),
        out_specs=pl.BlockSpec((1, tq, nh * SWA_HEAD_DIM), lambda i, t: (i, t, 0)),
        out_shape=jax.ShapeDtypeStruct((b, s, nh * SWA_HEAD_DIM), jnp.bfloat16),
        compiler_params=_params(2),
        name="window",
    )(sink, q, k, k, k, k_lead, v, v, v, v_lead)


def _mixout_kernel(x_ref, oa_ref, ob_ref, pre_g_ref, wg_ref, woa_ref, wob_ref, wout_ref,
                   post_g_ref, o_ref):
    x1 = x_ref[0]
    h = _rms(x1, pre_g_ref[...]).astype(jnp.bfloat16)
    y_a = _dot(oa_ref[0], woa_ref[...])
    y_b = _dot(ob_ref[0], wob_ref[...])
    merged = (jax.nn.sigmoid(_dot(h, wg_ref[:, :D_MODEL])) * y_a
              + jax.nn.sigmoid(_dot(h, wg_ref[:, D_MODEL:])) * y_b)
    o_ref[0] = x1 + _rms(_dot(merged.astype(jnp.bfloat16), wout_ref[...]), post_g_ref[...])


def _mixout(x1, o_a, o_b, pre_g, wg, woa, wob, wout, post_g, tm):
    b, s, d = x1.shape
    row = pl.BlockSpec((1, tm, d), lambda i, t: (i, t, 0))
    half = pl.BlockSpec((1, tm, o_a.shape[2]), lambda i, t: (i, t, 0))
    return pl.pallas_call(
        _mixout_kernel,
        grid=(b, s // tm),
        in_specs=[row, half, half, _const_spec(pre_g.shape), _const_spec(wg.shape),
                  _const_spec(woa.shape), _const_spec(wob.shape), _const_spec(wout.shape),
                  _const_spec(post_g.shape)],
        out_specs=row,
        out_shape=jax.ShapeDtypeStruct(x1.shape, jnp.float32),
        compiler_params=_params(2),
        name="mixout",
    )(x1, o_a, o_b, pre_g, wg, woa, wob, wout, post_g)


def _pack_weights(w_in, w_uq, w_ukv):
    bf16 = jnp.bfloat16
    d = w_in.shape[0]
    o = 0
    parts = []
    for n in (MLA_Q_LORA, MLA_KV_LORA, MLA_ROPE, SWA_HEADS * SWA_HEAD_DIM,
              SWA_KV_HEADS * SWA_HEAD_DIM, SWA_KV_HEADS * SWA_HEAD_DIM, D_MODEL, D_MODEL):
        parts.append(w_in[:, o:o + n])
        o += n
    w_cq, w_ckv, w_kr, w_qs, w_ks, w_vs, w_ga, w_gb = parts

    def mla_slab(nope, rope):
        half = MLA_ROPE // 2
        split = ROPE_PARTNER - half
        tail = jnp.zeros(nope.shape[:-1] + (LANES - MLA_NOPE - MLA_ROPE,), nope.dtype)
        return jnp.concatenate([rope[..., :half], nope[..., :split], rope[..., half:],
                                nope[..., split:], tail], axis=-1)

    def window_pairs(w):
        quarter = SWA_HEAD_DIM // 2
        w = w.reshape(d, -1, 2, 2, quarter)
        return jnp.swapaxes(w, 2, 3).reshape(d, -1)

    kr_slab = mla_slab(jnp.zeros((d, MLA_NOPE), w_kr.dtype), w_kr)
    w_proj = jnp.concatenate([w_cq, w_ckv, window_pairs(w_qs), window_pairs(w_ks), kr_slab],
                             axis=1).astype(bf16)
    w_gate = jnp.concatenate([w_ga, w_gb], axis=1).astype(bf16)
    w_vst = w_vs.T.astype(bf16)

    r = w_uq.shape[0]
    uq = w_uq.reshape(r, MLA_HEADS, MLA_NOPE + MLA_ROPE)
    wq = mla_slab(uq[..., :MLA_NOPE], uq[..., MLA_NOPE:]).reshape(r, -1).astype(bf16)

    r = w_ukv.shape[0]
    ukv = w_ukv.reshape(r, MLA_HEADS, MLA_NOPE + MLA_V)
    no_rope = jnp.zeros((r, MLA_HEADS, MLA_ROPE), w_ukv.dtype)
    wk = mla_slab(ukv[..., :MLA_NOPE], no_rope).reshape(r, -1).astype(bf16)
    zeros = jnp.zeros((r, MLA_HEADS, LANES - MLA_V), w_ukv.dtype)
    v_slab = jnp.concatenate([ukv[..., MLA_NOPE:], zeros], axis=-1)
    wvt = v_slab.reshape(r, -1).T.astype(bf16)
    return w_proj, w_gate, wq, wk, wvt, w_vst


def _rope_tables(pos):
    pos = pos.astype(jnp.float32)[:, None]
    lane = jnp.arange(LANES)[None, :]
    first = lane < ROPE_PARTNER

    def inv_freq(half):
        return ROPE_THETA ** (-jnp.arange(half, dtype=jnp.float32) / half)

    half_k = MLA_ROPE // 2
    ang_k = pos * inv_freq(half_k)[lane % half_k]
    is_rope = lane % ROPE_PARTNER < half_k
    ck = jnp.where(is_rope, jnp.cos(ang_k), 1.0)
    sk = jnp.where(is_rope, jnp.where(first, -jnp.sin(ang_k), jnp.sin(ang_k)), 0.0)
    half_w = SWA_HEAD_DIM // 2
    ang_w = pos * inv_freq(half_w)[lane % half_w]
    return ck, sk, jnp.cos(ang_w), jnp.where(first, -jnp.sin(ang_w), jnp.sin(ang_w))


def _row_tile(s):
    return min(s, 512)


def kernel(x_prompt, x_sample, meta_tokens, ffn1_pre_g, ffn1_w_in, ffn1_w_out, ffn1_post_g,
           mix_pre_g, w_in, q_norm_g, w_uq, kv_norm_g, w_ukv, sink, w_o_a, w_o_b, w_out,
           mix_post_g, ffn2_pre_g, ffn2_w_in, ffn2_w_out, ffn2_post_g):
    assert ffn1_w_in.shape[0] == 1, "single layer"
    bf16 = jnp.bfloat16
    f1 = (ffn1_pre_g, ffn1_w_in[0].astype(bf16), ffn1_w_out[0].astype(bf16), ffn1_post_g)
    f2 = (ffn2_pre_g, ffn2_w_in[0].astype(bf16), ffn2_w_out[0].astype(bf16), ffn2_post_g)
    w_proj, w_gate, wq, wk, wvt, w_vst = _pack_weights(w_in[0], w_uq[0], w_ukv[0])
    woa, wob, wout = w_o_a[0].astype(bf16), w_o_b[0].astype(bf16), w_out[0].astype(bf16)
    sink = sink[0]

    def front(x, tabs):
        tm = _row_tile(x.shape[1])
        x1 = _ffn(x, *f1, tm)
        return x1, _proj(x1, tabs, mix_pre_g, w_proj, w_vst, q_norm_g, wq, kv_norm_g, wk, wvt,
                         tm)

    lead = jnp.concatenate([jnp.zeros((LEAD_PAD, D_MODEL), x_prompt.dtype),
                            meta_tokens.astype(x_prompt.dtype)], axis=0)[None]
    _, (_, ka_lead, va_lead, _, kw_lead, vw_lead) = front(
        lead, _rope_tables(jnp.arange(BLOCK) - LEAD_PAD))
    seq_tabs = _rope_tables(jnp.arange(max(x_prompt.shape[1], x_sample.shape[1])) + N_META)

    def trunk(x):
        s = x.shape[1]
        tm = _row_tile(s)
        x1, (qa, ka, va, qw, kw, vw) = front(x, seq_tabs)
        o_a = _mla(qa, ka, va, ka_lead, va_lead, tq=MXU_COLS, ck=2048)
        o_b = _window(sink, qw, kw, vw, kw_lead, vw_lead, tq=4 * BLOCK)
        x2 = _mixout(x1, o_a, o_b, mix_pre_g, w_gate, woa, wob, wout, mix_post_g, tm)
        return _ffn(x2, *f2, tm)

    return (trunk(x_prompt), trunk(x_sample))
```

```python
import functools

import jax
import jax.numpy as jnp
from jax import lax
from jax.experimental import pallas as pl
from jax.experimental.pallas import tpu as pltpu

D_MODEL = 1024
N_META = 16
BLOCK = 128
WINDOW = 128
LEAD_PAD = BLOCK - N_META
ROPE_THETA = 10000.0
EPS = 1e-6
NEG = -1e30
MLA_HEADS = 8
MLA_Q_LORA = 384
MLA_KV_LORA = 256
MLA_NOPE = 64
MLA_ROPE = 32
MLA_V = 64
SWA_HEADS = 8
SWA_KV_HEADS = 2
SWA_HEAD_DIM = 64
SWA_GROUP = SWA_HEADS // SWA_KV_HEADS
D_FF = 2816

LANES = 128
ROPE_PARTNER = LANES // 2
MLA_SCALE = (MLA_NOPE + MLA_ROPE) ** -0.5
SWA_SCALE = SWA_HEAD_DIM ** -0.5
V7X_VMEM_BYTES = 64 * 1024 * 1024
VMEM_LIMIT = (V7X_VMEM_BYTES * 7) // 8

_C_Q = (0, MLA_Q_LORA)
_C_KV = (_C_Q[1], _C_Q[1] + MLA_KV_LORA)
_C_QS = (_C_KV[1], _C_KV[1] + SWA_HEADS * SWA_HEAD_DIM)
_C_KS = (_C_QS[1], _C_QS[1] + LANES)
_C_KR = (_C_KS[1], _C_KS[1] + LANES)
D_PROJ = _C_KR[1]

MXU_COLS = 256
FFN_CHUNKS = ((0, 6 * MXU_COLS), (6 * MXU_COLS, D_FF))
assert all((hi - lo) % MXU_COLS == 0 for lo, hi in FFN_CHUNKS)


def _const_spec(shape):
    zeros = (0,) * len(shape)
    return pl.BlockSpec(shape, lambda *_: zeros, pipeline_mode=pl.Buffered(1))


def _params(n_axes):
    return pltpu.CompilerParams(
        dimension_semantics=("parallel",) * n_axes, vmem_limit_bytes=VMEM_LIMIT)


def _rms(x, g):
    y = x * lax.rsqrt(jnp.mean(x * x, axis=-1, keepdims=True) + EPS)
    return y * g


def _dot(a, b):
    return jnp.dot(a, b, preferred_element_type=jnp.float32)


def _dot_nt(a, b):
    return lax.dot_general(a, b, (((1,), (1,)), ((), ())),
                           preferred_element_type=jnp.float32)


def _swiglu_residual(x, pre_g, w_in_ref, w_out_ref, post_g):
    h = _rms(x, pre_g).astype(jnp.bfloat16)
    acc = None
    for lo, hi in FFN_CHUNKS:
        g = _dot(h, w_in_ref[:, lo:hi])
        u = _dot(h, w_in_ref[:, D_FF + lo:D_FF + hi])
        a = (g * jax.nn.sigmoid(g) * u).astype(jnp.bfloat16)
        part = _dot(a, w_out_ref[lo:hi, :])
        acc = part if acc is None else acc + part
    return x + 0.5 * _rms(acc, post_g)


def _ffn_kernel(x_ref, pre_g_ref, w_in_ref, w_out_ref, post_g_ref, o_ref):
    half = x_ref.shape[1] // 2
    for rows in (slice(0, half), slice(half, 2 * half)):
        o_ref[0, rows, :] = _swiglu_residual(x_ref[0, rows, :], pre_g_ref[...], w_in_ref,
                                             w_out_ref, post_g_ref[...])


def _ffn(x, pre_g, w_in, w_out, post_g, tm):
    b, s, d = x.shape
    row = pl.BlockSpec((1, tm, d), lambda i, t: (i, t, 0))
    return pl.pallas_call(
        _ffn_kernel,
        grid=(b, s // tm),
        in_specs=[row, _const_spec(pre_g.shape), _const_spec(w_in.shape),
                  _const_spec(w_out.shape), _const_spec(post_g.shape)],
        out_specs=row,
        out_shape=jax.ShapeDtypeStruct(x.shape, jnp.float32),
        compiler_params=_params(2),
        name="ffn",
    )(x, pre_g, w_in, w_out, post_g)


def _proj_kernel(x_ref, ck_ref, sk_ref, cw_ref, sw_ref, pre_g_ref, w_ref, wvst_ref, qn_g_ref, wq_ref,
                 kvn_g_ref, wk_ref, wvt_ref, qa_ref, ka_ref, vt_ref, qw_ref, kw_ref, vw_ref):
    bf16 = jnp.bfloat16
    h = _rms(x_ref[0], pre_g_ref[...]).astype(bf16)
    ck_t, sk_t, cw_t, sw_t = ck_ref[...], sk_ref[...], cw_ref[...], sw_ref[...]
    lane = lax.broadcasted_iota(jnp.int32, (1, LANES), 1)

    proj = _dot(h, w_ref[...])

    def cols(rng):
        return proj[:, rng[0]:rng[1]]

    def rope(x, cos, sin_signed):
        return x * cos + pltpu.roll(x, ROPE_PARTNER, 1) * sin_signed

    qn = _rms(cols(_C_Q), qn_g_ref[...]).astype(bf16)
    qq = _dot(qn, wq_ref[...])
    for hd in range(MLA_HEADS):
        q_rope = rope(qq[:, hd * LANES:(hd + 1) * LANES], ck_t, sk_t)
        qa_ref[0, hd] = (q_rope * (MLA_SCALE * LOG2_E)).astype(bf16)

    kvn = _rms(cols(_C_KV), kvn_g_ref[...]).astype(bf16)
    kk = _dot(kvn, wk_ref[...])
    k_rope = rope(cols(_C_KR), ck_t, sk_t)
    for hd in range(MLA_HEADS):
        ka_ref[0, hd] = (kk[:, hd * LANES:(hd + 1) * LANES] + k_rope).astype(bf16)

    vt = _dot_nt(wvt_ref[...], kvn)
    is_v_row = lax.broadcasted_iota(jnp.int32, (LANES, 1), 0) < MLA_V
    for hd in range(MLA_HEADS):
        vt_ref[0, hd] = jnp.where(is_v_row, vt[hd * LANES:(hd + 1) * LANES], 1.0).astype(bf16)

    quarter = SWA_HEAD_DIM // 2
    even_lanes = lane % SWA_HEAD_DIM < quarter
    qs = cols(_C_QS)
    for pair in range(SWA_HEADS // 2):
        q_pair = rope(qs[:, pair * LANES:(pair + 1) * LANES], cw_t, sw_t) * (SWA_SCALE * LOG2_E)
        qw_ref[0, 2 * pair] = jnp.where(even_lanes, q_pair, 0.0).astype(bf16)
        qw_ref[0, 2 * pair + 1] = jnp.where(even_lanes, 0.0, q_pair).astype(bf16)

    k_pair = rope(cols(_C_KS), cw_t, sw_t)
    from_below = pltpu.roll(k_pair, quarter, 1)
    from_above = pltpu.roll(k_pair, LANES - quarter, 1)
    kw_ref[0, 0] = jnp.where(even_lanes, k_pair, from_below).astype(bf16)
    kw_ref[0, 1] = jnp.where(even_lanes, from_above, k_pair).astype(bf16)

    vtw = _dot_nt(wvst_ref[...], h)
    ones = jnp.ones((SWA_HEAD_DIM, vtw.shape[1]), jnp.float32)
    for g in range(SWA_KV_HEADS):
        v_g = vtw[g * SWA_HEAD_DIM:(g + 1) * SWA_HEAD_DIM]
        vw_ref[0, g] = jnp.concatenate([v_g, ones], axis=0).astype(bf16)


def _proj(x1, tabs, pre_g, w_proj, w_vst, qn_g, wq, kvn_g, wk, wvt, tm):
    b, s, d = x1.shape
    row = pl.BlockSpec((1, tm, d), lambda i, t: (i, t, 0))
    tab_spec = pl.BlockSpec((tm, LANES), lambda i, t: (t, 0))

    def heads(n):
        return (pl.BlockSpec((1, n, tm, LANES), lambda i, t: (i, 0, t, 0)),
                jax.ShapeDtypeStruct((b, n, s, LANES), jnp.bfloat16))

    def heads_t(n):
        return (pl.BlockSpec((1, n, LANES, tm), lambda i, t: (i, 0, 0, t)),
                jax.ShapeDtypeStruct((b, n, LANES, s), jnp.bfloat16))

    outs = (heads(MLA_HEADS), heads(MLA_HEADS), heads_t(MLA_HEADS),
            heads(SWA_HEADS), heads(SWA_KV_HEADS), heads_t(SWA_KV_HEADS))
    consts = (pre_g, w_proj, w_vst, qn_g, wq, kvn_g, wk, wvt)
    return pl.pallas_call(
        _proj_kernel,
        grid=(b, s // tm),
        in_specs=[row] + [tab_spec] * len(tabs) + [_const_spec(c.shape) for c in consts],
        out_specs=[o[0] for o in outs],
        out_shape=[o[1] for o in outs],
        compiler_params=_params(2),
        name="proj",
    )(x1, *tabs, *consts)


LOG2_E = 1.4426950408889634


def _mla_kernel(q_ref, k_ref, vt_ref, kl_ref, vtl_ref, o_ref, s_a, s_b, lead_a, lead_b, ot_scr,
                *, tq, ck):
    seq = k_ref.shape[2]
    n_chunks = seq // ck
    n_tiles = seq // tq
    sublanes = 8
    lead_ok = lax.broadcasted_iota(jnp.int32, (BLOCK, 1), 0) >= LEAD_PAD

    def fold_max(s):
        return jnp.max(s.reshape(s.shape[0] // sublanes, sublanes, tq), axis=0)

    def rows(t):
        return pl.ds(pl.multiple_of(t * tq, tq), tq)

    def score_chunk(j, q, c, s_new, mx):
        s = _dot_nt(k_ref[0, j, c * ck:(c + 1) * ck, :], q)
        s_new[c] = s
        return jnp.maximum(mx, fold_max(s))

    def value_chunk(j, c, s_old, m, acc):
        p = jnp.exp2(s_old[c] - m).astype(jnp.bfloat16)
        return acc + _dot(vt_ref[0, j, :, c * ck:(c + 1) * ck], p)

    def emit(j, t, acc):
        o_t = acc[:MLA_V] * (1.0 / acc[MLA_V:MLA_V + 1])
        if j % 2 == 0:
            ot_scr[t] = o_t
        else:
            both = jnp.concatenate([ot_scr[t], o_t], axis=0)
            slab = slice((j // 2) * LANES, (j // 2 + 1) * LANES)
            o_ref[0, rows(t), slab] = both.T.astype(o_ref.dtype)

    def lead_values(j, lead_old, mx_old):
        m = jnp.max(mx_old, axis=0, keepdims=True)
        return m, _dot(vtl_ref[0, j], jnp.exp2(lead_old[...] - m).astype(jnp.bfloat16))

    def stage(new, old, older, buf_new, buf_old, mx_old, acc_older):
        if older is not None:
            emit(*older, acc_older)
        j, t = new
        s_new, lead_new = buf_new
        q = q_ref[0, j, rows(t), :]
        s_lead = jnp.where(lead_ok, _dot_nt(kl_ref[0, j], q), NEG)
        lead_new[...] = s_lead
        mx, acc = fold_max(s_lead), None
        if old is not None:
            m, acc = lead_values(old[0], buf_old[1], mx_old)
        for c in range(n_chunks):
            mx = score_chunk(j, q, c, s_new, mx)
            if old is not None:
                acc = value_chunk(old[0], c, buf_old[0], m, acc)
        return mx, acc

    buf_a, buf_b = (s_a, lead_a), (s_b, lead_b)
    n_heads = q_ref.shape[1]
    end = n_tiles - 1
    for j in range(n_heads):
        if j == 0:
            mx, _ = stage((0, 0), None, None, buf_a, None, None, None)
            mx, acc = stage((0, 1), (0, 0), None, buf_b, buf_a, mx, None)
        else:
            mx, acc = stage((j, 0), (j - 1, end), (j - 1, end - 1), buf_a, buf_b, mx, acc)
            mx, acc = stage((j, 1), (j, 0), (j - 1, end), buf_b, buf_a, mx, acc)

        def pair(i, carry, j=j):
            t = 2 * i + 2
            mx, acc = stage((j, t), (j, t - 1), (j, t - 2), buf_a, buf_b, *carry)
            return stage((j, t + 1), (j, t), (j, t - 1), buf_b, buf_a, mx, acc)

        mx, acc = lax.fori_loop(0, n_tiles // 2 - 1, pair, (mx, acc))
    last = n_heads - 1
    emit(last, end - 1, acc)
    m, acc = lead_values(last, lead_b, mx)
    for c in range(n_chunks):
        acc = value_chunk(last, c, s_b, m, acc)
    emit(last, end, acc)


def _mla_heads_per_step(nh, s, tq, ck):
    bf16_bytes, f32_bytes, pipeline_buffers = 2, 4, 2
    scratch = 2 * (s + BLOCK) * tq * f32_bytes + s * MLA_V * f32_bytes
    in_flight = 2 * ck * tq * f32_bytes
    for hp in range(nh, 0, -2):
        per_head = (3 * s * LANES + s * MLA_V) * bf16_bytes
        if nh % hp == 0 and pipeline_buffers * hp * per_head + scratch + in_flight <= VMEM_LIMIT:
            return hp
    raise ValueError("sequence too long for the resident-key MLA kernel")


def _mla(q, k, vt, k_lead, vt_lead, tq, ck):
    b, nh, s, _ = q.shape
    hp = _mla_heads_per_step(nh, s, tq, ck)
    assert s % ck == 0 and s % (2 * tq) == 0
    seq_spec = pl.BlockSpec((1, hp, s, LANES), lambda i, p: (i, p, 0, 0))
    vt_spec = pl.BlockSpec((1, hp, LANES, s), lambda i, p: (i, p, 0, 0))
    lead_spec = pl.BlockSpec((1, hp, BLOCK, LANES), lambda i, p: (0, p, 0, 0))
    score_buf = pltpu.VMEM((s // ck, ck, tq), jnp.float32)
    lead_buf = pltpu.VMEM((BLOCK, tq), jnp.float32)
    return pl.pallas_call(
        functools.partial(_mla_kernel, tq=tq, ck=ck),
        grid=(b, nh // hp),
        in_specs=[seq_spec, seq_spec, vt_spec, lead_spec, lead_spec],
        out_specs=pl.BlockSpec((1, s, hp * MLA_V), lambda i, p: (i, 0, p)),
        out_shape=jax.ShapeDtypeStruct((b, s, nh * MLA_V), jnp.bfloat16),
        scratch_shapes=[score_buf, score_buf, lead_buf, lead_buf,
                        pltpu.VMEM((s // tq, MLA_V, tq), jnp.float32)],
        compiler_params=_params(2),
        name="mla",
    )(q, k, vt, k_lead, vt_lead)


def _window_kernel(sink_ref, q_ref, kl_ref, kc_ref, kr_ref, km_ref,
                   vl_ref, vc_ref, vr_ref, vm_ref, o_ref):
    tq = q_ref.shape[2]
    n_blk = tq // BLOCK
    t = pl.program_id(1)
    n_t = pl.num_programs(1)
    n_keys = 4 * BLOCK
    n_cols = SWA_GROUP * BLOCK
    key = lax.broadcasted_iota(jnp.int32, (n_keys, n_cols), 0)
    query = lax.broadcasted_iota(jnp.int32, (n_keys, n_cols), 1) % BLOCK
    in_left = jnp.logical_and(key < BLOCK, key >= query)
    in_right = jnp.logical_and(jnp.logical_and(key >= 2 * BLOCK, key < 3 * BLOCK),
                               key - 2 * BLOCK <= query)
    in_rest = jnp.logical_or(jnp.logical_and(key >= BLOCK, key < 2 * BLOCK),
                             key >= 3 * BLOCK + LEAD_PAD)
    sublanes = 8

    def keys_of(blk, left_ref, center_ref, right_ref, lead_ref, g):
        first, last = blk == 0, blk == n_blk - 1
        return jnp.concatenate(
            [left_ref[0, g] if first else center_ref[0, g, (blk - 1) * BLOCK:blk * BLOCK, :],
             center_ref[0, g, blk * BLOCK:(blk + 1) * BLOCK, :],
             right_ref[0, g] if last else center_ref[0, g, (blk + 1) * BLOCK:(blk + 2) * BLOCK, :],
             lead_ref[0, g]], axis=0)

    def values_t_of(blk, g):
        first, last = blk == 0, blk == n_blk - 1
        return jnp.concatenate(
            [vl_ref[0, g] if first else vc_ref[0, g, :, (blk - 1) * BLOCK:blk * BLOCK],
             vc_ref[0, g, :, blk * BLOCK:(blk + 1) * BLOCK],
             vr_ref[0, g] if last else vc_ref[0, g, :, (blk + 1) * BLOCK:(blk + 2) * BLOCK],
             vm_ref[0, g]], axis=1)

    def scores(g, blk):
        ok_left = jnp.logical_and(in_left, t > 0) if blk == 0 else in_left
        ok_right = jnp.logical_and(in_right, t < n_t - 1) if blk == n_blk - 1 else in_right
        ok = jnp.logical_or(jnp.logical_or(ok_left, ok_right), in_rest)
        q = q_ref[0, g * SWA_GROUP:(g + 1) * SWA_GROUP, blk * BLOCK:(blk + 1) * BLOCK, :]
        s = _dot_nt(keys_of(blk, kl_ref, kc_ref, kr_ref, km_ref, g),
                    q.reshape(n_cols, LANES))
        return jnp.where(ok, s, NEG)

    def finish(g, blk, s):
        sink = jnp.concatenate(
            [jnp.full((1, BLOCK), sink_ref[g * SWA_GROUP + r] * LOG2_E, jnp.float32)
             for r in range(SWA_GROUP)], axis=1)
        folded = jnp.max(s.reshape(n_keys // sublanes, sublanes, n_cols), axis=0)
        m = jnp.maximum(jnp.max(folded, axis=0, keepdims=True), sink)
        p = jnp.exp2(s - m).astype(jnp.bfloat16)
        acc = _dot(values_t_of(blk, g), p)
        o_t = acc[:SWA_HEAD_DIM] * (
            1.0 / (acc[SWA_HEAD_DIM:SWA_HEAD_DIM + 1] + jnp.exp2(sink - m)))
        for pair in range(SWA_GROUP // 2):
            both = jnp.concatenate(
                [o_t[:, (2 * pair) * BLOCK:(2 * pair + 1) * BLOCK],
                 o_t[:, (2 * pair + 1) * BLOCK:(2 * pair + 2) * BLOCK]], axis=0)
            slab = g * (SWA_GROUP // 2) + pair
            o_ref[0, blk * BLOCK:(blk + 1) * BLOCK, slab * LANES:(slab + 1) * LANES] = (
                both.T.astype(o_ref.dtype))

    units = [(g, blk) for g in range(SWA_KV_HEADS) for blk in range(n_blk)]
    s_next = scores(*units[0])
    for i, unit in enumerate(units):
        s_cur = s_next
        if i + 1 < len(units):
            s_next = scores(*units[i + 1])
        finish(*unit, s_cur)


def _window(sink, q, k, v, k_lead, v_lead, tq):
    b, nh, s, _ = q.shape
    n_blk = s // BLOCK
    per = tq // BLOCK

    def specs(n, transposed):
        def spec(rows, index):
            if transposed:
                return pl.BlockSpec((1, n, LANES, rows), lambda i, t: (*index(i, t)[:2], 0,
                                                                       index(i, t)[2]))
            return pl.BlockSpec((1, n, rows, LANES), lambda i, t: (*index(i, t), 0))

        return [spec(BLOCK, lambda i, t: (i, 0, jnp.maximum(t * per - 1, 0))),
                spec(tq, lambda i, t: (i, 0, t)),
                spec(BLOCK, lambda i, t: (i, 0, jnp.minimum((t + 1) * per, n_blk - 1))),
                spec(BLOCK, lambda i, t: (0, 0, 0))]

    smem = pl.BlockSpec(memory_space=pltpu.SMEM)
    return pl.pallas_call(
        _window_kernel,
        grid=(b, s // tq),
        in_specs=([smem, specs(nh, False)[1]] + specs(SWA_KV_HEADS, False)
                  + specs(SWA_KV_HEADS, True)),
        out_specs=pl.BlockSpec((1, tq, nh * SWA_HEAD_DIM), lambda i, t: (i, t, 0)),
        out_shape=jax.ShapeDtypeStruct((b, s, nh * SWA_HEAD_DIM), jnp.bfloat16),
        compiler_params=_params(2),
        name="window",
    )(sink, q, k, k, k, k_lead, v, v, v, v_lead)


def _mixout_kernel(x_ref, oa_ref, ob_ref, pre_g_ref, wg_ref, woa_ref, wob_ref, wout_ref,
                   post_g_ref, o_ref):
    half = x_ref.shape[1] // 2
    for rows in (slice(0, half), slice(half, 2 * half)):
        x1 = x_ref[0, rows, :]
        h = _rms(x1, pre_g_ref[...]).astype(jnp.bfloat16)
        y_a = _dot(oa_ref[0, rows, :], woa_ref[...])
        y_b = _dot(ob_ref[0, rows, :], wob_ref[...])
        merged = (jax.nn.sigmoid(_dot(h, wg_ref[:, :D_MODEL])) * y_a
                  + jax.nn.sigmoid(_dot(h, wg_ref[:, D_MODEL:])) * y_b)
        o_ref[0, rows, :] = x1 + _rms(_dot(merged.astype(jnp.bfloat16), wout_ref[...]),
                                      post_g_ref[...])


def _mixout(x1, o_a, o_b, pre_g, wg, woa, wob, wout, post_g, tm):
    b, s, d = x1.shape
    row = pl.BlockSpec((1, tm, d), lambda i, t: (i, t, 0))
    half = pl.BlockSpec((1, tm, o_a.shape[2]), lambda i, t: (i, t, 0))
    return pl.pallas_call(
        _mixout_kernel,
        grid=(b, s // tm),
        in_specs=[row, half, half, _const_spec(pre_g.shape), _const_spec(wg.shape),
                  _const_spec(woa.shape), _const_spec(wob.shape), _const_spec(wout.shape),
                  _const_spec(post_g.shape)],
        out_specs=row,
        out_shape=jax.ShapeDtypeStruct(x1.shape, jnp.float32),
        compiler_params=_params(2),
        name="mixout",
    )(x1, o_a, o_b, pre_g, wg, woa, wob, wout, post_g)


def _pack_weights(w_in, w_uq, w_ukv):
    bf16 = jnp.bfloat16
    d = w_in.shape[0]
    o = 0
    parts = []
    for n in (MLA_Q_LORA, MLA_KV_LORA, MLA_ROPE, SWA_HEADS * SWA_HEAD_DIM,
              SWA_KV_HEADS * SWA_HEAD_DIM, SWA_KV_HEADS * SWA_HEAD_DIM, D_MODEL, D_MODEL):
        parts.append(w_in[:, o:o + n])
        o += n
    w_cq, w_ckv, w_kr, w_qs, w_ks, w_vs, w_ga, w_gb = parts

    def mla_slab(nope, rope):
        half = MLA_ROPE // 2
        split = ROPE_PARTNER - half
        tail = jnp.zeros(nope.shape[:-1] + (LANES - MLA_NOPE - MLA_ROPE,), nope.dtype)
        return jnp.concatenate([rope[..., :half], nope[..., :split], rope[..., half:],
                                nope[..., split:], tail], axis=-1)

    def window_pairs(w):
        quarter = SWA_HEAD_DIM // 2
        w = w.reshape(d, -1, 2, 2, quarter)
        return jnp.swapaxes(w, 2, 3).reshape(d, -1)

    kr_slab = mla_slab(jnp.zeros((d, MLA_NOPE), w_kr.dtype), w_kr)
    w_proj = jnp.concatenate([w_cq, w_ckv, window_pairs(w_qs), window_pairs(w_ks), kr_slab],
                             axis=1).astype(bf16)
    w_gate = jnp.concatenate([w_ga, w_gb], axis=1).astype(bf16)
    w_vst = w_vs.T.astype(bf16)

    r = w_uq.shape[0]
    uq = w_uq.reshape(r, MLA_HEADS, MLA_NOPE + MLA_ROPE)
    wq = mla_slab(uq[..., :MLA_NOPE], uq[..., MLA_NOPE:]).reshape(r, -1).astype(bf16)

    r = w_ukv.shape[0]
    ukv = w_ukv.reshape(r, MLA_HEADS, MLA_NOPE + MLA_V)
    no_rope = jnp.zeros((r, MLA_HEADS, MLA_ROPE), w_ukv.dtype)
    wk = mla_slab(ukv[..., :MLA_NOPE], no_rope).reshape(r, -1).astype(bf16)
    zeros = jnp.zeros((r, MLA_HEADS, LANES - MLA_V), w_ukv.dtype)
    v_slab = jnp.concatenate([ukv[..., MLA_NOPE:], zeros], axis=-1)
    wvt = v_slab.reshape(r, -1).T.astype(bf16)
    return w_proj, w_gate, wq, wk, wvt, w_vst


def _rope_tables(pos):
    pos = pos.astype(jnp.float32)[:, None]
    lane = jnp.arange(LANES)[None, :]
    first = lane < ROPE_PARTNER

    def inv_freq(half):
        return ROPE_THETA ** (-jnp.arange(half, dtype=jnp.float32) / half)

    half_k = MLA_ROPE // 2
    ang_k = pos * inv_freq(half_k)[lane % half_k]
    is_rope = lane % ROPE_PARTNER < half_k
    ck = jnp.where(is_rope, jnp.cos(ang_k), 1.0)
    sk = jnp.where(is_rope, jnp.where(first, -jnp.sin(ang_k), jnp.sin(ang_k)), 0.0)
    half_w = SWA_HEAD_DIM // 2
    ang_w = pos * inv_freq(half_w)[lane % half_w]
    return ck, sk, jnp.cos(ang_w), jnp.where(first, -jnp.sin(ang_w), jnp.sin(ang_w))


def _row_tile(s):
    return min(s, 512)


def kernel(x_prompt, x_sample, meta_tokens, ffn1_pre_g, ffn1_w_in, ffn1_w_out, ffn1_post_g,
           mix_pre_g, w_in, q_norm_g, w_uq, kv_norm_g, w_ukv, sink, w_o_a, w_o_b, w_out,
           mix_post_g, ffn2_pre_g, ffn2_w_in, ffn2_w_out, ffn2_post_g):
    assert ffn1_w_in.shape[0] == 1, "single layer"
    bf16 = jnp.bfloat16
    f1 = (ffn1_pre_g, ffn1_w_in[0].astype(bf16), ffn1_w_out[0].astype(bf16), ffn1_post_g)
    f2 = (ffn2_pre_g, ffn2_w_in[0].astype(bf16), ffn2_w_out[0].astype(bf16), ffn2_post_g)
    w_proj, w_gate, wq, wk, wvt, w_vst = _pack_weights(w_in[0], w_uq[0], w_ukv[0])
    woa, wob, wout = w_o_a[0].astype(bf16), w_o_b[0].astype(bf16), w_out[0].astype(bf16)
    sink = sink[0]

    def front(x, tabs):
        tm = _row_tile(x.shape[1])
        x1 = _ffn(x, *f1, tm)
        return x1, _proj(x1, tabs, mix_pre_g, w_proj, w_vst, q_norm_g, wq, kv_norm_g, wk, wvt,
                         tm)

    lead = jnp.concatenate([jnp.zeros((LEAD_PAD, D_MODEL), x_prompt.dtype),
                            meta_tokens.astype(x_prompt.dtype)], axis=0)[None]
    _, (_, ka_lead, va_lead, _, kw_lead, vw_lead) = front(
        lead, _rope_tables(jnp.arange(BLOCK) - LEAD_PAD))
    seq_tabs = _rope_tables(jnp.arange(max(x_prompt.shape[1], x_sample.shape[1])) + N_META)

    def trunk(x):
        s = x.shape[1]
        tm = _row_tile(s)
        x1, (qa, ka, va, qw, kw, vw) = front(x, seq_tabs)
        o_a = _mla(qa, ka, va, ka_lead, va_lead, tq=MXU_COLS, ck=2048)
        o_b = _window(sink, qw, kw, vw, kw_lead, vw_lead, tq=4 * BLOCK)
        x2 = _mixout(x1, o_a, o_b, mix_pre_g, w_gate, woa, wob, wout, mix_post_g, tm)
        return _ffn(x2, *f2, tm)

    return (trunk(x_prompt), trunk(x_sample))
```

```python
import functools

import jax
import jax.numpy as jnp
from jax import lax
from jax.experimental import pallas as pl
from jax.experimental.pallas import tpu as pltpu

D_MODEL = 1024
N_META = 16
BLOCK = 128
WINDOW = 128
LEAD_PAD = BLOCK - N_META
ROPE_THETA = 10000.0
EPS = 1e-6
NEG = -1e30
MLA_HEADS = 8
MLA_Q_LORA = 384
MLA_KV_LORA = 256
MLA_NOPE = 64
MLA_ROPE = 32
MLA_V = 64
SWA_HEADS = 8
SWA_KV_HEADS = 2
SWA_HEAD_DIM = 64
SWA_GROUP = SWA_HEADS // SWA_KV_HEADS
D_FF = 2816

LANES = 128
ROPE_PARTNER = LANES // 2
MLA_SCALE = (MLA_NOPE + MLA_ROPE) ** -0.5
SWA_SCALE = SWA_HEAD_DIM ** -0.5
V7X_VMEM_BYTES = 64 * 1024 * 1024
VMEM_LIMIT = (V7X_VMEM_BYTES * 7) // 8

_C_Q = (0, MLA_Q_LORA)
_C_KV = (_C_Q[1], _C_Q[1] + MLA_KV_LORA)
_C_QS = (_C_KV[1], _C_KV[1] + SWA_HEADS * SWA_HEAD_DIM)
_C_KS = (_C_QS[1], _C_QS[1] + LANES)
_C_KR = (_C_KS[1], _C_KS[1] + LANES)
D_PROJ = _C_KR[1]

MXU_COLS = 256
FFN_CHUNKS = ((0, 6 * MXU_COLS), (6 * MXU_COLS, D_FF))
assert all((hi - lo) % MXU_COLS == 0 for lo, hi in FFN_CHUNKS)


def _const_spec(shape):
    zeros = (0,) * len(shape)
    return pl.BlockSpec(shape, lambda *_: zeros, pipeline_mode=pl.Buffered(1))


def _params(n_axes):
    return pltpu.CompilerParams(
        dimension_semantics=("parallel",) * n_axes, vmem_limit_bytes=VMEM_LIMIT)


def _rms(x, g):
    y = x * lax.rsqrt(jnp.mean(x * x, axis=-1, keepdims=True) + EPS)
    return y * g


def _dot(a, b):
    return jnp.dot(a, b, preferred_element_type=jnp.float32)


def _dot_nt(a, b):
    return lax.dot_general(a, b, (((1,), (1,)), ((), ())),
                           preferred_element_type=jnp.float32)


def _swiglu_residual(x, pre_g, w_in_ref, w_out_ref, post_g):
    h = _rms(x, pre_g).astype(jnp.bfloat16)
    acc = None
    for lo, hi in FFN_CHUNKS:
        g = _dot(h, w_in_ref[:, lo:hi])
        u = _dot(h, w_in_ref[:, D_FF + lo:D_FF + hi])
        a = (g * jax.nn.sigmoid(g) * u).astype(jnp.bfloat16)
        part = _dot(a, w_out_ref[lo:hi, :])
        acc = part if acc is None else acc + part
    return x + 0.5 * _rms(acc, post_g)


def _ffn_kernel(x_ref, pre_g_ref, w_in_ref, w_out_ref, post_g_ref, o_ref):
    half = x_ref.shape[1] // 2
    for rows in (slice(0, half), slice(half, 2 * half)):
        o_ref[0, rows, :] = _swiglu_residual(x_ref[0, rows, :], pre_g_ref[...], w_in_ref,
                                             w_out_ref, post_g_ref[...])


def _ffn(x, pre_g, w_in, w_out, post_g, tm):
    b, s, d = x.shape
    row = pl.BlockSpec((1, tm, d), lambda i, t: (i, t, 0))
    return pl.pallas_call(
        _ffn_kernel,
        grid=(b, s // tm),
        in_specs=[row, _const_spec(pre_g.shape), _const_spec(w_in.shape),
                  _const_spec(w_out.shape), _const_spec(post_g.shape)],
        out_specs=row,
        out_shape=jax.ShapeDtypeStruct(x.shape, jnp.float32),
        compiler_params=_params(2),
        name="ffn",
    )(x, pre_g, w_in, w_out, post_g)


def _proj_kernel(x_ref, ck_ref, sk_ref, cw_ref, sw_ref, pre_g_ref, w_ref, wvst_ref, qn_g_ref, wq_ref,
                 kvn_g_ref, wk_ref, wvt_ref, qa_ref, ka_ref, vt_ref, qw_ref, kw_ref, vw_ref):
    bf16 = jnp.bfloat16
    h = _rms(x_ref[0], pre_g_ref[...]).astype(bf16)
    ck_t, sk_t, cw_t, sw_t = ck_ref[...], sk_ref[...], cw_ref[...], sw_ref[...]
    lane = lax.broadcasted_iota(jnp.int32, (1, LANES), 1)

    proj = _dot(h, w_ref[...])

    def cols(rng):
        return proj[:, rng[0]:rng[1]]

    def rope(x, cos, sin_signed):
        return x * cos + pltpu.roll(x, ROPE_PARTNER, 1) * sin_signed

    qn = _rms(cols(_C_Q), qn_g_ref[...]).astype(bf16)
    qq = _dot(qn, wq_ref[...])
    for hd in range(MLA_HEADS):
        q_rope = rope(qq[:, hd * LANES:(hd + 1) * LANES], ck_t, sk_t)
        qa_ref[0, hd] = (q_rope * (MLA_SCALE * LOG2_E)).astype(bf16)

    kvn = _rms(cols(_C_KV), kvn_g_ref[...]).astype(bf16)
    kk = _dot(kvn, wk_ref[...])
    k_rope = rope(cols(_C_KR), ck_t, sk_t)
    for hd in range(MLA_HEADS):
        ka_ref[0, hd] = (kk[:, hd * LANES:(hd + 1) * LANES] + k_rope).astype(bf16)

    vt = _dot_nt(wvt_ref[...], kvn)
    is_v_row = lax.broadcasted_iota(jnp.int32, (LANES, 1), 0) < MLA_V
    for hd in range(MLA_HEADS):
        vt_ref[0, hd] = jnp.where(is_v_row, vt[hd * LANES:(hd + 1) * LANES], 1.0).astype(bf16)

    quarter = SWA_HEAD_DIM // 2
    even_lanes = lane % SWA_HEAD_DIM < quarter
    qs = cols(_C_QS)
    for pair in range(SWA_HEADS // 2):
        q_pair = rope(qs[:, pair * LANES:(pair + 1) * LANES], cw_t, sw_t) * (SWA_SCALE * LOG2_E)
        qw_ref[0, 2 * pair] = jnp.where(even_lanes, q_pair, 0.0).astype(bf16)
        qw_ref[0, 2 * pair + 1] = jnp.where(even_lanes, 0.0, q_pair).astype(bf16)

    k_pair = rope(cols(_C_KS), cw_t, sw_t)
    from_below = pltpu.roll(k_pair, quarter, 1)
    from_above = pltpu.roll(k_pair, LANES - quarter, 1)
    kw_ref[0, 0] = jnp.where(even_lanes, k_pair, from_below).astype(bf16)
    kw_ref[0, 1] = jnp.where(even_lanes, from_above, k_pair).astype(bf16)

    vtw = _dot_nt(wvst_ref[...], h)
    ones = jnp.ones((SWA_HEAD_DIM, vtw.shape[1]), jnp.float32)
    for g in range(SWA_KV_HEADS):
        v_g = vtw[g * SWA_HEAD_DIM:(g + 1) * SWA_HEAD_DIM]
        vw_ref[0, g] = jnp.concatenate([v_g, ones], axis=0).astype(bf16)


def _proj(x1, tabs, pre_g, w_proj, w_vst, qn_g, wq, kvn_g, wk, wvt, tm):
    b, s, d = x1.shape
    row = pl.BlockSpec((1, tm, d), lambda i, t: (i, t, 0))
    tab_spec = pl.BlockSpec((tm, LANES), lambda i, t: (t, 0))

    def heads(n):
        return (pl.BlockSpec((1, n, tm, LANES), lambda i, t: (i, 0, t, 0)),
                jax.ShapeDtypeStruct((b, n, s, LANES), jnp.bfloat16))

    def heads_t(n):
        return (pl.BlockSpec((1, n, LANES, tm), lambda i, t: (i, 0, 0, t)),
                jax.ShapeDtypeStruct((b, n, LANES, s), jnp.bfloat16))

    outs = (heads(MLA_HEADS), heads(MLA_HEADS), heads_t(MLA_HEADS),
            heads(SWA_HEADS), heads(SWA_KV_HEADS), heads_t(SWA_KV_HEADS))
    consts = (pre_g, w_proj, w_vst, qn_g, wq, kvn_g, wk, wvt)
    return pl.pallas_call(
        _proj_kernel,
        grid=(b, s // tm),
        in_specs=[row] + [tab_spec] * len(tabs) + [_const_spec(c.shape) for c in consts],
        out_specs=[o[0] for o in outs],
        out_shape=[o[1] for o in outs],
        compiler_params=_params(2),
        name="proj",
    )(x1, *tabs, *consts)


LOG2_E = 1.4426950408889634


def _mla_kernel(q_ref, k_ref, vt_ref, kl_ref, vtl_ref, o_ref, s_a, s_b, lead_a, lead_b, ot_scr,
                *, tq, ck):
    seq = k_ref.shape[2]
    n_chunks = seq // ck
    n_tiles = seq // tq
    sublanes = 8
    lead_ok = lax.broadcasted_iota(jnp.int32, (BLOCK, 1), 0) >= LEAD_PAD

    def fold_max(s):
        return jnp.max(s.reshape(s.shape[0] // sublanes, sublanes, tq), axis=0)

    def rows(t):
        return pl.ds(pl.multiple_of(t * tq, tq), tq)

    def score_chunk(j, q, c, s_new, mx):
        s = _dot_nt(k_ref[0, j, c * ck:(c + 1) * ck, :], q)
        s_new[c] = s
        return jnp.maximum(mx, fold_max(s))

    def value_chunk(j, c, s_old, m, acc):
        p = jnp.exp2(s_old[c] - m).astype(jnp.bfloat16)
        return acc + _dot(vt_ref[0, j, :, c * ck:(c + 1) * ck], p)

    def emit(j, t, acc):
        o_t = acc[:MLA_V] * (1.0 / acc[MLA_V:MLA_V + 1])
        if j % 2 == 0:
            ot_scr[t] = o_t
        else:
            both = jnp.concatenate([ot_scr[t], o_t], axis=0)
            slab = slice((j // 2) * LANES, (j // 2 + 1) * LANES)
            o_ref[0, rows(t), slab] = both.T.astype(o_ref.dtype)

    def lead_values(j, lead_old, mx_old):
        m = jnp.max(mx_old, axis=0, keepdims=True)
        return m, _dot(vtl_ref[0, j], jnp.exp2(lead_old[...] - m).astype(jnp.bfloat16))

    def stage(new, old, older, buf_new, buf_old, mx_old, acc_older):
        if older is not None:
            emit(*older, acc_older)
        j, t = new
        s_new, lead_new = buf_new
        q = q_ref[0, j, rows(t), :]
        s_lead = jnp.where(lead_ok, _dot_nt(kl_ref[0, j], q), NEG)
        lead_new[...] = s_lead
        mx, acc = fold_max(s_lead), None
        if old is not None:
            m, acc = lead_values(old[0], buf_old[1], mx_old)
        for c in range(n_chunks):
            mx = score_chunk(j, q, c, s_new, mx)
            if old is not None:
                acc = value_chunk(old[0], c, buf_old[0], m, acc)
        return mx, acc

    buf_a, buf_b = (s_a, lead_a), (s_b, lead_b)
    n_heads = q_ref.shape[1]
    end = n_tiles - 1
    for j in range(n_heads):
        if j == 0:
            mx, _ = stage((0, 0), None, None, buf_a, None, None, None)
            mx, acc = stage((0, 1), (0, 0), None, buf_b, buf_a, mx, None)
        else:
            mx, acc = stage((j, 0), (j - 1, end), (j - 1, end - 1), buf_a, buf_b, mx, acc)
            mx, acc = stage((j, 1), (j, 0), (j - 1, end), buf_b, buf_a, mx, acc)

        def pair(i, carry, j=j):
            t = 2 * i + 2
            mx, acc = stage((j, t), (j, t - 1), (j, t - 2), buf_a, buf_b, *carry)
            return stage((j, t + 1), (j, t), (j, t - 1), buf_b, buf_a, mx, acc)

        mx, acc = lax.fori_loop(0, n_tiles // 2 - 1, pair, (mx, acc))
    last = n_heads - 1
    emit(last, end - 1, acc)
    m, acc = lead_values(last, lead_b, mx)
    for c in range(n_chunks):
        acc = value_chunk(last, c, s_b, m, acc)
    emit(last, end, acc)


def _mla_heads_per_step(nh, s, tq, ck):
    bf16_bytes, f32_bytes, pipeline_buffers = 2, 4, 2
    scratch = 2 * (s + BLOCK) * tq * f32_bytes + s * MLA_V * f32_bytes
    in_flight = 2 * ck * tq * f32_bytes
    for hp in range(nh, 0, -2):
        per_head = (3 * s * LANES + s * MLA_V) * bf16_bytes
        if nh % hp == 0 and pipeline_buffers * hp * per_head + scratch + in_flight <= VMEM_LIMIT:
            return hp
    raise ValueError("sequence too long for the resident-key MLA kernel")


def _mla(q, k, vt, k_lead, vt_lead, tq, ck):
    b, nh, s, _ = q.shape
    hp = _mla_heads_per_step(nh, s, tq, ck)
    assert s % ck == 0 and s % (2 * tq) == 0
    seq_spec = pl.BlockSpec((1, hp, s, LANES), lambda i, p: (i, p, 0, 0))
    vt_spec = pl.BlockSpec((1, hp, LANES, s), lambda i, p: (i, p, 0, 0))
    lead_spec = pl.BlockSpec((1, hp, BLOCK, LANES), lambda i, p: (0, p, 0, 0))
    score_buf = pltpu.VMEM((s // ck, ck, tq), jnp.float32)
    lead_buf = pltpu.VMEM((BLOCK, tq), jnp.float32)
    return pl.pallas_call(
        functools.partial(_mla_kernel, tq=tq, ck=ck),
        grid=(b, nh // hp),
        in_specs=[seq_spec, seq_spec, vt_spec, lead_spec, lead_spec],
        out_specs=pl.BlockSpec((1, s, hp * MLA_V), lambda i, p: (i, 0, p)),
        out_shape=jax.ShapeDtypeStruct((b, s, nh * MLA_V), jnp.bfloat16),
        scratch_shapes=[score_buf, score_buf, lead_buf, lead_buf,
                        pltpu.VMEM((s // tq, MLA_V, tq), jnp.float32)],
        compiler_params=_params(2),
        name="mla",
    )(q, k, vt, k_lead, vt_lead)


def _window_kernel(sink_ref, q_ref, kl_ref, kc_ref, kr_ref, km_ref,
                   vl_ref, vc_ref, vr_ref, vm_ref, o_ref):
    tq = q_ref.shape[2]
    n_blk = tq // BLOCK
    t = pl.program_id(1)
    n_t = pl.num_programs(1)
    n_keys = 4 * BLOCK
    n_cols = SWA_GROUP * BLOCK
    key = lax.broadcasted_iota(jnp.int32, (n_keys, n_cols), 0)
    query = lax.broadcasted_iota(jnp.int32, (n_keys, n_cols), 1) % BLOCK
    in_left = jnp.logical_and(key < BLOCK, key >= query)
    in_right = jnp.logical_and(jnp.logical_and(key >= 2 * BLOCK, key < 3 * BLOCK),
                               key - 2 * BLOCK <= query)
    in_rest = jnp.logical_or(jnp.logical_and(key >= BLOCK, key < 2 * BLOCK),
                             key >= 3 * BLOCK + LEAD_PAD)
    sublanes = 8

    def keys_of(blk, left_ref, center_ref, right_ref, lead_ref, g):
        first, last = blk == 0, blk == n_blk - 1
        return jnp.concatenate(
            [left_ref[0, g] if first else center_ref[0, g, (blk - 1) * BLOCK:blk * BLOCK, :],
             center_ref[0, g, blk * BLOCK:(blk + 1) * BLOCK, :],
             right_ref[0, g] if last else center_ref[0, g, (blk + 1) * BLOCK:(blk + 2) * BLOCK, :],
             lead_ref[0, g]], axis=0)

    def values_t_of(blk, g):
        first, last = blk == 0, blk == n_blk - 1
        return jnp.concatenate(
            [vl_ref[0, g] if first else vc_ref[0, g, :, (blk - 1) * BLOCK:blk * BLOCK],
             vc_ref[0, g, :, blk * BLOCK:(blk + 1) * BLOCK],
             vr_ref[0, g] if last else vc_ref[0, g, :, (blk + 1) * BLOCK:(blk + 2) * BLOCK],
             vm_ref[0, g]], axis=1)

    def scores(g, blk):
        ok_left = jnp.logical_and(in_left, t > 0) if blk == 0 else in_left
        ok_right = jnp.logical_and(in_right, t < n_t - 1) if blk == n_blk - 1 else in_right
        ok = jnp.logical_or(jnp.logical_or(ok_left, ok_right), in_rest)
        q = q_ref[0, g * SWA_GROUP:(g + 1) * SWA_GROUP, blk * BLOCK:(blk + 1) * BLOCK, :]
        s = _dot_nt(keys_of(blk, kl_ref, kc_ref, kr_ref, km_ref, g),
                    q.reshape(n_cols, LANES))
        return jnp.where(ok, s, NEG)

    def finish(g, blk, s):
        sink = jnp.concatenate(
            [jnp.full((1, BLOCK), sink_ref[g * SWA_GROUP + r] * LOG2_E, jnp.float32)
             for r in range(SWA_GROUP)], axis=1)
        folded = jnp.max(s.reshape(n_keys // sublanes, sublanes, n_cols), axis=0)
        m = jnp.maximum(jnp.max(folded, axis=0, keepdims=True), sink)
        p = jnp.exp2(s - m).astype(jnp.bfloat16)
        acc = _dot(values_t_of(blk, g), p)
        o_t = acc[:SWA_HEAD_DIM] * (
            1.0 / (acc[SWA_HEAD_DIM:SWA_HEAD_DIM + 1] + jnp.exp2(sink - m)))
        for pair in range(SWA_GROUP // 2):
            both = jnp.concatenate(
                [o_t[:, (2 * pair) * BLOCK:(2 * pair + 1) * BLOCK],
                 o_t[:, (2 * pair + 1) * BLOCK:(2 * pair + 2) * BLOCK]], axis=0)
            slab = g * (SWA_GROUP // 2) + pair
            o_ref[0, blk * BLOCK:(blk + 1) * BLOCK, slab * LANES:(slab + 1) * LANES] = (
                both.T.astype(o_ref.dtype))

    units = [(g, blk) for g in range(SWA_KV_HEADS) for blk in range(n_blk)]
    ahead = 2
    pending = [scores(*u) for u in units[:ahead]]
    for i, unit in enumerate(units):
        if i + ahead < len(units):
            pending.append(scores(*units[i + ahead]))
        finish(*unit, pending.pop(0))


def _window(sink, q, k, v, k_lead, v_lead, tq):
    b, nh, s, _ = q.shape
    n_blk = s // BLOCK
    per = tq // BLOCK

    def specs(n, transposed):
        def spec(rows, index):
            if transposed:
                return pl.BlockSpec((1, n, LANES, rows), lambda i, t: (*index(i, t)[:2], 0,
                                                                       index(i, t)[2]))
            return pl.BlockSpec((1, n, rows, LANES), lambda i, t: (*index(i, t), 0))

        return [spec(BLOCK, lambda i, t: (i, 0, jnp.maximum(t * per - 1, 0))),
                spec(tq, lambda i, t: (i, 0, t)),
                spec(BLOCK, lambda i, t: (i, 0, jnp.minimum((t + 1) * per, n_blk - 1))),
                spec(BLOCK, lambda i, t: (0, 0, 0))]

    smem = pl.BlockSpec(memory_space=pltpu.SMEM)
    return pl.pallas_call(
        _window_kernel,
        grid=(b, s // tq),
        in_specs=([smem, specs(nh, False)[1]] + specs(SWA_KV_HEADS, False)
                  + specs(SWA_KV_HEADS, True)),
        out_specs=pl.BlockSpec((1, tq, nh * SWA_HEAD_DIM), lambda i, t: (i, t, 0)),
        out_shape=jax.ShapeDtypeStruct((b, s, nh * SWA_HEAD_DIM), jnp.bfloat16),
        compiler_params=_params(2),
        name="window",
    )(sink, q, k, k, k, k_lead, v, v, v, v_lead)


def _mixout_kernel(x_ref, oa_ref, ob_ref, pre_g_ref, wg_ref, woa_ref, wob_ref, wout_ref,
                   post_g_ref, o_ref):
    half = x_ref.shape[1] // 2
    for rows in (slice(0, half), slice(half, 2 * half)):
        x1 = x_ref[0, rows, :]
        h = _rms(x1, pre_g_ref[...]).astype(jnp.bfloat16)
        y_a = _dot(oa_ref[0, rows, :], woa_ref[...])
        y_b = _dot(ob_ref[0, rows, :], wob_ref[...])
        merged = (jax.nn.sigmoid(_dot(h, wg_ref[:, :D_MODEL])) * y_a
                  + jax.nn.sigmoid(_dot(h, wg_ref[:, D_MODEL:])) * y_b)
        o_ref[0, rows, :] = x1 + _rms(_dot(merged.astype(jnp.bfloat16), wout_ref[...]),
                                      post_g_ref[...])


def _mixout(x1, o_a, o_b, pre_g, wg, woa, wob, wout, post_g, tm):
    b, s, d = x1.shape
    row = pl.BlockSpec((1, tm, d), lambda i, t: (i, t, 0))
    half = pl.BlockSpec((1, tm, o_a.shape[2]), lambda i, t: (i, t, 0))
    return pl.pallas_call(
        _mixout_kernel,
        grid=(b, s // tm),
        in_specs=[row, half, half, _const_spec(pre_g.shape), _const_spec(wg.shape),
                  _const_spec(woa.shape), _const_spec(wob.shape), _const_spec(wout.shape),
                  _const_spec(post_g.shape)],
        out_specs=row,
        out_shape=jax.ShapeDtypeStruct(x1.shape, jnp.float32),
        compiler_params=_params(2),
        name="mixout",
    )(x1, o_a, o_b, pre_g, wg, woa, wob, wout, post_g)


def _pack_weights(w_in, w_uq, w_ukv):
    bf16 = jnp.bfloat16
    d = w_in.shape[0]
    o = 0
    parts = []
    for n in (MLA_Q_LORA, MLA_KV_LORA, MLA_ROPE, SWA_HEADS * SWA_HEAD_DIM,
              SWA_KV_HEADS * SWA_HEAD_DIM, SWA_KV_HEADS * SWA_HEAD_DIM, D_MODEL, D_MODEL):
        parts.append(w_in[:, o:o + n])
        o += n
    w_cq, w_ckv, w_kr, w_qs, w_ks, w_vs, w_ga, w_gb = parts

    def mla_slab(nope, rope):
        half = MLA_ROPE // 2
        split = ROPE_PARTNER - half
        tail = jnp.zeros(nope.shape[:-1] + (LANES - MLA_NOPE - MLA_ROPE,), nope.dtype)
        return jnp.concatenate([rope[..., :half], nope[..., :split], rope[..., half:],
                                nope[..., split:], tail], axis=-1)

    def window_pairs(w):
        quarter = SWA_HEAD_DIM // 2
        w = w.reshape(d, -1, 2, 2, quarter)
        return jnp.swapaxes(w, 2, 3).reshape(d, -1)

    kr_slab = mla_slab(jnp.zeros((d, MLA_NOPE), w_kr.dtype), w_kr)
    w_proj = jnp.concatenate([w_cq, w_ckv, window_pairs(w_qs), window_pairs(w_ks), kr_slab],
                             axis=1).astype(bf16)
    w_gate = jnp.concatenate([w_ga, w_gb], axis=1).astype(bf16)
    w_vst = w_vs.T.astype(bf16)

    r = w_uq.shape[0]
    uq = w_uq.reshape(r, MLA_HEADS, MLA_NOPE + MLA_ROPE)
    wq = mla_slab(uq[..., :MLA_NOPE], uq[..., MLA_NOPE:]).reshape(r, -1).astype(bf16)

    r = w_ukv.shape[0]
    ukv = w_ukv.reshape(r, MLA_HEADS, MLA_NOPE + MLA_V)
    no_rope = jnp.zeros((r, MLA_HEADS, MLA_ROPE), w_ukv.dtype)
    wk = mla_slab(ukv[..., :MLA_NOPE], no_rope).reshape(r, -1).astype(bf16)
    zeros = jnp.zeros((r, MLA_HEADS, LANES - MLA_V), w_ukv.dtype)
    v_slab = jnp.concatenate([ukv[..., MLA_NOPE:], zeros], axis=-1)
    wvt = v_slab.reshape(r, -1).T.astype(bf16)
    return w_proj, w_gate, wq, wk, wvt, w_vst


def _rope_tables(pos):
    pos = pos.astype(jnp.float32)[:, None]
    lane = jnp.arange(LANES)[None, :]
    first = lane < ROPE_PARTNER

    def inv_freq(half):
        return ROPE_THETA ** (-jnp.arange(half, dtype=jnp.float32) / half)

    half_k = MLA_ROPE // 2
    ang_k = pos * inv_freq(half_k)[lane % half_k]
    is_rope = lane % ROPE_PARTNER < half_k
    ck = jnp.where(is_rope, jnp.cos(ang_k), 1.0)
    sk = jnp.where(is_rope, jnp.where(first, -jnp.sin(ang_k), jnp.sin(ang_k)), 0.0)
    half_w = SWA_HEAD_DIM // 2
    ang_w = pos * inv_freq(half_w)[lane % half_w]
    return ck, sk, jnp.cos(ang_w), jnp.where(first, -jnp.sin(ang_w), jnp.sin(ang_w))


def _row_tile(s):
    return min(s, 512)


def kernel(x_prompt, x_sample, meta_tokens, ffn1_pre_g, ffn1_w_in, ffn1_w_out, ffn1_post_g,
           mix_pre_g, w_in, q_norm_g, w_uq, kv_norm_g, w_ukv, sink, w_o_a, w_o_b, w_out,
           mix_post_g, ffn2_pre_g, ffn2_w_in, ffn2_w_out, ffn2_post_g):
    assert ffn1_w_in.shape[0] == 1, "single layer"
    bf16 = jnp.bfloat16
    f1 = (ffn1_pre_g, ffn1_w_in[0].astype(bf16), ffn1_w_out[0].astype(bf16), ffn1_post_g)
    f2 = (ffn2_pre_g, ffn2_w_in[0].astype(bf16), ffn2_w_out[0].astype(bf16), ffn2_post_g)
    w_proj, w_gate, wq, wk, wvt, w_vst = _pack_weights(w_in[0], w_uq[0], w_ukv[0])
    woa, wob, wout = w_o_a[0].astype(bf16), w_o_b[0].astype(bf16), w_out[0].astype(bf16)
    sink = sink[0]

    def front(x, tabs):
        tm = _row_tile(x.shape[1])
        x1 = _ffn(x, *f1, tm)
        return x1, _proj(x1, tabs, mix_pre_g, w_proj, w_vst, q_norm_g, wq, kv_norm_g, wk, wvt,
                         tm)

    lead = jnp.concatenate([jnp.zeros((LEAD_PAD, D_MODEL), x_prompt.dtype),
                            meta_tokens.astype(x_prompt.dtype)], axis=0)[None]
    _, (_, ka_lead, va_lead, _, kw_lead, vw_lead) = front(
        lead, _rope_tables(jnp.arange(BLOCK) - LEAD_PAD))
    seq_tabs = _rope_tables(jnp.arange(max(x_prompt.shape[1], x_sample.shape[1])) + N_META)

    def trunk(x):
        s = x.shape[1]
        tm = _row_tile(s)
        x1, (qa, ka, va, qw, kw, vw) = front(x, seq_tabs)
        o_a = _mla(qa, ka, va, ka_lead, va_lead, tq=MXU_COLS, ck=1024)
        o_b = _window(sink, qw, kw, vw, kw_lead, vw_lead, tq=4 * BLOCK)
        x2 = _mixout(x1, o_a, o_b, mix_pre_g, w_gate, woa, wob, wout, mix_post_g, tm)
        return _ffn(x2, *f2, tm)

    return (trunk(x_prompt), trunk(x_sample))
```

```python
import functools

import jax
import jax.numpy as jnp
from jax import lax
from jax.experimental import pallas as pl
from jax.experimental.pallas import tpu as pltpu

D_MODEL = 1024
N_META = 16
BLOCK = 128
WINDOW = 128
assert WINDOW == BLOCK, "the band masks below are written for a window of one block"
LEAD_PAD = BLOCK - N_META
ROPE_THETA = 10000.0
EPS = 1e-6
NEG = -1e30
MLA_HEADS = 8
MLA_Q_LORA = 384
MLA_KV_LORA = 256
MLA_NOPE = 64
MLA_ROPE = 32
MLA_V = 64
SWA_HEADS = 8
SWA_KV_HEADS = 2
SWA_HEAD_DIM = 64
SWA_GROUP = SWA_HEADS // SWA_KV_HEADS
D_FF = 2816

LANES = 128
ROPE_PARTNER = LANES // 2
MLA_SCALE = (MLA_NOPE + MLA_ROPE) ** -0.5
SWA_SCALE = SWA_HEAD_DIM ** -0.5
V7X_VMEM_BYTES = 64 * 1024 * 1024
VMEM_LIMIT = (V7X_VMEM_BYTES * 7) // 8

_C_Q = (0, MLA_Q_LORA)
_C_KV = (_C_Q[1], _C_Q[1] + MLA_KV_LORA)
_C_QS = (_C_KV[1], _C_KV[1] + SWA_HEADS * SWA_HEAD_DIM)
_C_KS = (_C_QS[1], _C_QS[1] + LANES)
_C_KR = (_C_KS[1], _C_KS[1] + LANES)

MXU_COLS = 256
FFN_CHUNKS = ((0, 6 * MXU_COLS), (6 * MXU_COLS, D_FF))
assert all((hi - lo) % MXU_COLS == 0 for lo, hi in FFN_CHUNKS)


def _const_spec(shape):
    zeros = (0,) * len(shape)
    return pl.BlockSpec(shape, lambda *_: zeros, pipeline_mode=pl.Buffered(1))


def _params(n_axes):
    return pltpu.CompilerParams(
        dimension_semantics=("parallel",) * n_axes, vmem_limit_bytes=VMEM_LIMIT)


def _rms(x, g):
    y = x * lax.rsqrt(jnp.mean(x * x, axis=-1, keepdims=True) + EPS)
    return y * g


def _dot(a, b):
    return jnp.dot(a, b, preferred_element_type=jnp.float32)


def _dot_nt(a, b):
    return lax.dot_general(a, b, (((1,), (1,)), ((), ())),
                           preferred_element_type=jnp.float32)


def _swiglu_residual(x, pre_g, w_in_ref, w_out_ref, post_g):
    h = _rms(x, pre_g).astype(jnp.bfloat16)
    acc = None
    for lo, hi in FFN_CHUNKS:
        g = _dot(h, w_in_ref[:, lo:hi])
        u = _dot(h, w_in_ref[:, D_FF + lo:D_FF + hi])
        a = (g * jax.nn.sigmoid(g) * u).astype(jnp.bfloat16)
        part = _dot(a, w_out_ref[lo:hi, :])
        acc = part if acc is None else acc + part
    return x + 0.5 * _rms(acc, post_g)


def _ffn_kernel(x_ref, pre_g_ref, w_in_ref, w_out_ref, post_g_ref, o_ref):
    half = x_ref.shape[1] // 2
    for rows in (slice(0, half), slice(half, 2 * half)):
        o_ref[0, rows, :] = _swiglu_residual(x_ref[0, rows, :], pre_g_ref[...], w_in_ref,
                                             w_out_ref, post_g_ref[...])


def _ffn(x, pre_g, w_in, w_out, post_g, tm):
    b, s, d = x.shape
    row = pl.BlockSpec((1, tm, d), lambda i, t: (i, t, 0))
    return pl.pallas_call(
        _ffn_kernel,
        grid=(b, s // tm),
        in_specs=[row, _const_spec(pre_g.shape), _const_spec(w_in.shape),
                  _const_spec(w_out.shape), _const_spec(post_g.shape)],
        out_specs=row,
        out_shape=jax.ShapeDtypeStruct(x.shape, jnp.float32),
        compiler_params=_params(2),
        name="ffn",
    )(x, pre_g, w_in, w_out, post_g)


def _proj_kernel(x_ref, ck_ref, sk_ref, cw_ref, sw_ref, pre_g_ref, w_ref, wvst_ref, qn_g_ref,
                 wq_ref, kvn_g_ref, wk_ref, wvt_ref,
                 qa_ref, ka_ref, vt_ref, qw_ref, kw_ref, vw_ref):
    bf16 = jnp.bfloat16
    h = _rms(x_ref[0], pre_g_ref[...]).astype(bf16)
    ck_t, sk_t, cw_t, sw_t = ck_ref[...], sk_ref[...], cw_ref[...], sw_ref[...]
    lane = lax.broadcasted_iota(jnp.int32, (1, LANES), 1)

    proj = _dot(h, w_ref[...])

    def cols(rng):
        return proj[:, rng[0]:rng[1]]

    def rope(x, cos, sin_signed):
        return x * cos + pltpu.roll(x, ROPE_PARTNER, 1) * sin_signed

    qn = _rms(cols(_C_Q), qn_g_ref[...]).astype(bf16)
    qq = _dot(qn, wq_ref[...])
    for hd in range(MLA_HEADS):
        q_rope = rope(qq[:, hd * LANES:(hd + 1) * LANES], ck_t, sk_t)
        qa_ref[0, hd] = (q_rope * (MLA_SCALE * LOG2_E)).astype(bf16)

    kvn = _rms(cols(_C_KV), kvn_g_ref[...]).astype(bf16)
    kk = _dot(kvn, wk_ref[...])
    k_rope = rope(cols(_C_KR), ck_t, sk_t)
    for hd in range(MLA_HEADS):
        ka_ref[0, hd] = (kk[:, hd * LANES:(hd + 1) * LANES] + k_rope).astype(bf16)

    vt = _dot_nt(wvt_ref[...], kvn)
    is_v_row = lax.broadcasted_iota(jnp.int32, (LANES, 1), 0) < MLA_V
    for hd in range(MLA_HEADS):
        vt_ref[0, hd] = jnp.where(is_v_row, vt[hd * LANES:(hd + 1) * LANES], 1.0).astype(bf16)

    quarter = SWA_HEAD_DIM // 2
    even_lanes = lane % SWA_HEAD_DIM < quarter
    qs = cols(_C_QS)
    for pair in range(SWA_HEADS // 2):
        q_pair = rope(qs[:, pair * LANES:(pair + 1) * LANES], cw_t, sw_t) * (SWA_SCALE * LOG2_E)
        qw_ref[0, 2 * pair] = jnp.where(even_lanes, q_pair, 0.0).astype(bf16)
        qw_ref[0, 2 * pair + 1] = jnp.where(even_lanes, 0.0, q_pair).astype(bf16)

    k_pair = rope(cols(_C_KS), cw_t, sw_t)
    from_below = pltpu.roll(k_pair, quarter, 1)
    from_above = pltpu.roll(k_pair, LANES - quarter, 1)
    kw_ref[0, 0] = jnp.where(even_lanes, k_pair, from_below).astype(bf16)
    kw_ref[0, 1] = jnp.where(even_lanes, from_above, k_pair).astype(bf16)

    vtw = _dot_nt(wvst_ref[...], h)
    ones = jnp.ones((SWA_HEAD_DIM, vtw.shape[1]), jnp.float32)
    for g in range(SWA_KV_HEADS):
        v_g = vtw[g * SWA_HEAD_DIM:(g + 1) * SWA_HEAD_DIM]
        vw_ref[0, g] = jnp.concatenate([v_g, ones], axis=0).astype(bf16)


def _proj(x1, tabs, pre_g, w_proj, w_vst, qn_g, wq, kvn_g, wk, wvt, tm):
    b, s, d = x1.shape
    row = pl.BlockSpec((1, tm, d), lambda i, t: (i, t, 0))
    tab_spec = pl.BlockSpec((tm, LANES), lambda i, t: (t, 0))

    def heads(n):
        return (pl.BlockSpec((1, n, tm, LANES), lambda i, t: (i, 0, t, 0)),
                jax.ShapeDtypeStruct((b, n, s, LANES), jnp.bfloat16))

    def heads_t(n):
        return (pl.BlockSpec((1, n, LANES, tm), lambda i, t: (i, 0, 0, t)),
                jax.ShapeDtypeStruct((b, n, LANES, s), jnp.bfloat16))

    outs = (heads(MLA_HEADS), heads(MLA_HEADS), heads_t(MLA_HEADS),
            heads(SWA_HEADS), heads(SWA_KV_HEADS), heads_t(SWA_KV_HEADS))
    consts = (pre_g, w_proj, w_vst, qn_g, wq, kvn_g, wk, wvt)
    return pl.pallas_call(
        _proj_kernel,
        grid=(b, s // tm),
        in_specs=[row] + [tab_spec] * len(tabs) + [_const_spec(c.shape) for c in consts],
        out_specs=[o[0] for o in outs],
        out_shape=[o[1] for o in outs],
        compiler_params=_params(2),
        name="proj",
    )(x1, *tabs, *consts)


LOG2_E = 1.4426950408889634


def _mla_kernel(q_ref, k_ref, vt_ref, kl_ref, vtl_ref, o_ref, s_a, s_b, lead_a, lead_b, ot_scr,
                *, tq, ck):
    seq = k_ref.shape[2]
    n_chunks = seq // ck
    n_tiles = seq // tq
    sublanes = 8
    lead_ok = lax.broadcasted_iota(jnp.int32, (BLOCK, 1), 0) >= LEAD_PAD

    def fold_max(s):
        return jnp.max(s.reshape(s.shape[0] // sublanes, sublanes, tq), axis=0)

    def rows(t):
        return pl.ds(pl.multiple_of(t * tq, tq), tq)

    def score_chunk(j, q, c, s_new, mx):
        s = _dot_nt(k_ref[0, j, c * ck:(c + 1) * ck, :], q)
        s_new[c] = s
        return jnp.maximum(mx, fold_max(s))

    def value_chunk(j, c, s_old, m, acc):
        p = jnp.exp2(s_old[c] - m).astype(jnp.bfloat16)
        return acc + _dot(vt_ref[0, j, :, c * ck:(c + 1) * ck], p)

    def emit(j, t, acc):
        o_t = acc[:MLA_V] * (1.0 / acc[MLA_V:MLA_V + 1])
        if j % 2 == 0:
            ot_scr[t] = o_t
        else:
            both = jnp.concatenate([ot_scr[t], o_t], axis=0)
            slab = slice((j // 2) * LANES, (j // 2 + 1) * LANES)
            o_ref[0, rows(t), slab] = both.T.astype(o_ref.dtype)

    def lead_values(j, lead_old, mx_old):
        m = jnp.max(mx_old, axis=0, keepdims=True)
        return m, _dot(vtl_ref[0, j], jnp.exp2(lead_old[...] - m).astype(jnp.bfloat16))

    def stage(new, old, older, buf_new, buf_old, mx_old, acc_older):
        if older is not None:
            emit(*older, acc_older)
        j, t = new
        s_new, lead_new = buf_new
        q = q_ref[0, j, rows(t), :]
        s_lead = jnp.where(lead_ok, _dot_nt(kl_ref[0, j], q), NEG)
        lead_new[...] = s_lead
        mx, acc = fold_max(s_lead), None
        if old is not None:
            m, acc = lead_values(old[0], buf_old[1], mx_old)
        for c in range(n_chunks):
            mx = score_chunk(j, q, c, s_new, mx)
            if old is not None:
                acc = value_chunk(old[0], c, buf_old[0], m, acc)
        return mx, acc

    buf_a, buf_b = (s_a, lead_a), (s_b, lead_b)
    n_heads = q_ref.shape[1]
    end = n_tiles - 1
    for j in range(n_heads):
        if j == 0:
            mx, _ = stage((0, 0), None, None, buf_a, None, None, None)
            mx, acc = stage((0, 1), (0, 0), None, buf_b, buf_a, mx, None)
        else:
            mx, acc = stage((j, 0), (j - 1, end), (j - 1, end - 1), buf_a, buf_b, mx, acc)
            mx, acc = stage((j, 1), (j, 0), (j - 1, end), buf_b, buf_a, mx, acc)

        def pair(i, carry, j=j):
            t = 2 * i + 2
            mx, acc = stage((j, t), (j, t - 1), (j, t - 2), buf_a, buf_b, *carry)
            return stage((j, t + 1), (j, t), (j, t - 1), buf_b, buf_a, mx, acc)

        mx, acc = lax.fori_loop(0, n_tiles // 2 - 1, pair, (mx, acc))
    last = n_heads - 1
    emit(last, end - 1, acc)
    m, acc = lead_values(last, lead_b, mx)
    for c in range(n_chunks):
        acc = value_chunk(last, c, s_b, m, acc)
    emit(last, end, acc)


def _mla_heads_per_step(nh, s, tq, ck):
    bf16_bytes, f32_bytes, pipeline_buffers = 2, 4, 2
    scratch = 2 * (s + BLOCK) * tq * f32_bytes + s * MLA_V * f32_bytes
    in_flight = 2 * ck * tq * f32_bytes
    for hp in range(nh, 0, -2):
        per_head = (3 * s * LANES + s * MLA_V) * bf16_bytes
        if nh % hp == 0 and pipeline_buffers * hp * per_head + scratch + in_flight <= VMEM_LIMIT:
            return hp
    raise ValueError("sequence too long for the resident-key MLA kernel")


def _mla(q, k, vt, k_lead, vt_lead, tq, ck):
    b, nh, s, _ = q.shape
    hp = _mla_heads_per_step(nh, s, tq, ck)
    assert s % ck == 0 and s % (2 * tq) == 0
    seq_spec = pl.BlockSpec((1, hp, s, LANES), lambda i, p: (i, p, 0, 0))
    vt_spec = pl.BlockSpec((1, hp, LANES, s), lambda i, p: (i, p, 0, 0))
    lead_spec = pl.BlockSpec((1, hp, BLOCK, LANES), lambda i, p: (0, p, 0, 0))
    score_buf = pltpu.VMEM((s // ck, ck, tq), jnp.float32)
    lead_buf = pltpu.VMEM((BLOCK, tq), jnp.float32)
    return pl.pallas_call(
        functools.partial(_mla_kernel, tq=tq, ck=ck),
        grid=(b, nh // hp),
        in_specs=[seq_spec, seq_spec, vt_spec, lead_spec, lead_spec],
        out_specs=pl.BlockSpec((1, s, hp * MLA_V), lambda i, p: (i, 0, p)),
        out_shape=jax.ShapeDtypeStruct((b, s, nh * MLA_V), jnp.bfloat16),
        scratch_shapes=[score_buf, score_buf, lead_buf, lead_buf,
                        pltpu.VMEM((s // tq, MLA_V, tq), jnp.float32)],
        compiler_params=_params(2),
        name="mla",
    )(q, k, vt, k_lead, vt_lead)


def _window_kernel(sink_ref, q_ref, kl_ref, kc_ref, kr_ref, km_ref,
                   vl_ref, vc_ref, vr_ref, vm_ref, o_ref):
    tq = q_ref.shape[2]
    n_blk = tq // BLOCK
    t = pl.program_id(1)
    n_t = pl.num_programs(1)
    n_keys = 4 * BLOCK
    n_cols = SWA_GROUP * BLOCK
    key = lax.broadcasted_iota(jnp.int32, (n_keys, n_cols), 0)
    query = lax.broadcasted_iota(jnp.int32, (n_keys, n_cols), 1) % BLOCK
    in_left = jnp.logical_and(key < BLOCK, key >= query)
    in_right = jnp.logical_and(jnp.logical_and(key >= 2 * BLOCK, key < 3 * BLOCK),
                               key - 2 * BLOCK <= query)
    in_rest = jnp.logical_or(jnp.logical_and(key >= BLOCK, key < 2 * BLOCK),
                             key >= 3 * BLOCK + LEAD_PAD)
    sublanes = 8

    def keys_of(blk, left_ref, center_ref, right_ref, lead_ref, g):
        first, last = blk == 0, blk == n_blk - 1
        return jnp.concatenate(
            [left_ref[0, g] if first else center_ref[0, g, (blk - 1) * BLOCK:blk * BLOCK, :],
             center_ref[0, g, blk * BLOCK:(blk + 1) * BLOCK, :],
             right_ref[0, g] if last else center_ref[0, g, (blk + 1) * BLOCK:(blk + 2) * BLOCK, :],
             lead_ref[0, g]], axis=0)

    def values_t_of(blk, g):
        first, last = blk == 0, blk == n_blk - 1
        return jnp.concatenate(
            [vl_ref[0, g] if first else vc_ref[0, g, :, (blk - 1) * BLOCK:blk * BLOCK],
             vc_ref[0, g, :, blk * BLOCK:(blk + 1) * BLOCK],
             vr_ref[0, g] if last else vc_ref[0, g, :, (blk + 1) * BLOCK:(blk + 2) * BLOCK],
             vm_ref[0, g]], axis=1)

    def scores(g, blk):
        ok_left = jnp.logical_and(in_left, t > 0) if blk == 0 else in_left
        ok_right = jnp.logical_and(in_right, t < n_t - 1) if blk == n_blk - 1 else in_right
        ok = jnp.logical_or(jnp.logical_or(ok_left, ok_right), in_rest)
        q = q_ref[0, g * SWA_GROUP:(g + 1) * SWA_GROUP, blk * BLOCK:(blk + 1) * BLOCK, :]
        s = _dot_nt(keys_of(blk, kl_ref, kc_ref, kr_ref, km_ref, g),
                    q.reshape(n_cols, LANES))
        return jnp.where(ok, s, NEG)

    def finish(g, blk, s):
        sink = jnp.concatenate(
            [jnp.full((1, BLOCK), sink_ref[g * SWA_GROUP + r] * LOG2_E, jnp.float32)
             for r in range(SWA_GROUP)], axis=1)
        folded = jnp.max(s.reshape(n_keys // sublanes, sublanes, n_cols), axis=0)
        m = jnp.maximum(jnp.max(folded, axis=0, keepdims=True), sink)
        p = jnp.exp2(s - m).astype(jnp.bfloat16)
        acc = _dot(values_t_of(blk, g), p)
        o_t = acc[:SWA_HEAD_DIM] * (
            1.0 / (acc[SWA_HEAD_DIM:SWA_HEAD_DIM + 1] + jnp.exp2(sink - m)))
        for pair in range(SWA_GROUP // 2):
            both = jnp.concatenate(
                [o_t[:, (2 * pair) * BLOCK:(2 * pair + 1) * BLOCK],
                 o_t[:, (2 * pair + 1) * BLOCK:(2 * pair + 2) * BLOCK]], axis=0)
            slab = g * (SWA_GROUP // 2) + pair
            o_ref[0, blk * BLOCK:(blk + 1) * BLOCK, slab * LANES:(slab + 1) * LANES] = (
                both.T.astype(o_ref.dtype))

    units = [(g, blk) for g in range(SWA_KV_HEADS) for blk in range(n_blk)]
    ahead = 3
    pending = [scores(*u) for u in units[:ahead]]
    for i, unit in enumerate(units):
        if i + ahead < len(units):
            pending.append(scores(*units[i + ahead]))
        finish(*unit, pending.pop(0))


def _window(sink, q, k, v, k_lead, v_lead, tq):
    b, nh, s, _ = q.shape
    n_blk = s // BLOCK
    per = tq // BLOCK

    def specs(n, transposed):
        def spec(rows, index):
            if transposed:
                return pl.BlockSpec((1, n, LANES, rows), lambda i, t: (*index(i, t)[:2], 0,
                                                                       index(i, t)[2]))
            return pl.BlockSpec((1, n, rows, LANES), lambda i, t: (*index(i, t), 0))

        return [spec(BLOCK, lambda i, t: (i, 0, jnp.maximum(t * per - 1, 0))),
                spec(tq, lambda i, t: (i, 0, t)),
                spec(BLOCK, lambda i, t: (i, 0, jnp.minimum((t + 1) * per, n_blk - 1))),
                spec(BLOCK, lambda i, t: (0, 0, 0))]

    smem = pl.BlockSpec(memory_space=pltpu.SMEM)
    return pl.pallas_call(
        _window_kernel,
        grid=(b, s // tq),
        in_specs=([smem, specs(nh, False)[1]] + specs(SWA_KV_HEADS, False)
                  + specs(SWA_KV_HEADS, True)),
        out_specs=pl.BlockSpec((1, tq, nh * SWA_HEAD_DIM), lambda i, t: (i, t, 0)),
        out_shape=jax.ShapeDtypeStruct((b, s, nh * SWA_HEAD_DIM), jnp.bfloat16),
        compiler_params=_params(2),
        name="window",
    )(sink, q, k, k, k, k_lead, v, v, v, v_lead)


def _mixout_kernel(x_ref, oa_ref, ob_ref, pre_g_ref, wg_ref, woa_ref, wob_ref, wout_ref,
                   post_g_ref, o_ref):
    half = x_ref.shape[1] // 2
    for rows in (slice(0, half), slice(half, 2 * half)):
        x1 = x_ref[0, rows, :]
        h = _rms(x1, pre_g_ref[...]).astype(jnp.bfloat16)
        y_a = _dot(oa_ref[0, rows, :], woa_ref[...])
        y_b = _dot(ob_ref[0, rows, :], wob_ref[...])
        merged = (jax.nn.sigmoid(_dot(h, wg_ref[:, :D_MODEL])) * y_a
                  + jax.nn.sigmoid(_dot(h, wg_ref[:, D_MODEL:])) * y_b)
        o_ref[0, rows, :] = x1 + _rms(_dot(merged.astype(jnp.bfloat16), wout_ref[...]),
                                      post_g_ref[...])


def _mixout(x1, o_a, o_b, pre_g, wg, woa, wob, wout, post_g, tm):
    b, s, d = x1.shape
    row = pl.BlockSpec((1, tm, d), lambda i, t: (i, t, 0))
    half = pl.BlockSpec((1, tm, o_a.shape[2]), lambda i, t: (i, t, 0))
    return pl.pallas_call(
        _mixout_kernel,
        grid=(b, s // tm),
        in_specs=[row, half, half, _const_spec(pre_g.shape), _const_spec(wg.shape),
                  _const_spec(woa.shape), _const_spec(wob.shape), _const_spec(wout.shape),
                  _const_spec(post_g.shape)],
        out_specs=row,
        out_shape=jax.ShapeDtypeStruct(x1.shape, jnp.float32),
        compiler_params=_params(2),
        name="mixout",
    )(x1, o_a, o_b, pre_g, wg, woa, wob, wout, post_g)


def _pack_weights(w_in, w_uq, w_ukv):
    bf16 = jnp.bfloat16
    d = w_in.shape[0]
    o = 0
    parts = []
    for n in (MLA_Q_LORA, MLA_KV_LORA, MLA_ROPE, SWA_HEADS * SWA_HEAD_DIM,
              SWA_KV_HEADS * SWA_HEAD_DIM, SWA_KV_HEADS * SWA_HEAD_DIM, D_MODEL, D_MODEL):
        parts.append(w_in[:, o:o + n])
        o += n
    w_cq, w_ckv, w_kr, w_qs, w_ks, w_vs, w_ga, w_gb = parts

    def mla_slab(nope, rope):
        half = MLA_ROPE // 2
        split = ROPE_PARTNER - half
        tail = jnp.zeros(nope.shape[:-1] + (LANES - MLA_NOPE - MLA_ROPE,), nope.dtype)
        return jnp.concatenate([rope[..., :half], nope[..., :split], rope[..., half:],
                                nope[..., split:], tail], axis=-1)

    def window_pairs(w):
        quarter = SWA_HEAD_DIM // 2
        w = w.reshape(d, -1, 2, 2, quarter)
        return jnp.swapaxes(w, 2, 3).reshape(d, -1)

    kr_slab = mla_slab(jnp.zeros((d, MLA_NOPE), w_kr.dtype), w_kr)
    w_proj = jnp.concatenate([w_cq, w_ckv, window_pairs(w_qs), window_pairs(w_ks), kr_slab],
                             axis=1).astype(bf16)
    w_gate = jnp.concatenate([w_ga, w_gb], axis=1).astype(bf16)
    w_vst = w_vs.T.astype(bf16)

    r = w_uq.shape[0]
    uq = w_uq.reshape(r, MLA_HEADS, MLA_NOPE + MLA_ROPE)
    wq = mla_slab(uq[..., :MLA_NOPE], uq[..., MLA_NOPE:]).reshape(r, -1).astype(bf16)

    r = w_ukv.shape[0]
    ukv = w_ukv.reshape(r, MLA_HEADS, MLA_NOPE + MLA_V)
    no_rope = jnp.zeros((r, MLA_HEADS, MLA_ROPE), w_ukv.dtype)
    wk = mla_slab(ukv[..., :MLA_NOPE], no_rope).reshape(r, -1).astype(bf16)
    zeros = jnp.zeros((r, MLA_HEADS, LANES - MLA_V), w_ukv.dtype)
    v_slab = jnp.concatenate([ukv[..., MLA_NOPE:], zeros], axis=-1)
    wvt = v_slab.reshape(r, -1).T.astype(bf16)
    return w_proj, w_gate, wq, wk, wvt, w_vst


def _rope_tables(pos):
    pos = pos.astype(jnp.float32)[:, None]
    lane = jnp.arange(LANES)[None, :]
    first = lane < ROPE_PARTNER

    def inv_freq(half):
        return ROPE_THETA ** (-jnp.arange(half, dtype=jnp.float32) / half)

    half_k = MLA_ROPE // 2
    ang_k = pos * inv_freq(half_k)[lane % half_k]
    is_rope = lane % ROPE_PARTNER < half_k
    ck = jnp.where(is_rope, jnp.cos(ang_k), 1.0)
    sk = jnp.where(is_rope, jnp.where(first, -jnp.sin(ang_k), jnp.sin(ang_k)), 0.0)
    half_w = SWA_HEAD_DIM // 2
    ang_w = pos * inv_freq(half_w)[lane % half_w]
    return ck, sk, jnp.cos(ang_w), jnp.where(first, -jnp.sin(ang_w), jnp.sin(ang_w))


def _row_tile(s):
    return min(s, 512)


def kernel(x_prompt, x_sample, meta_tokens, ffn1_pre_g, ffn1_w_in, ffn1_w_out, ffn1_post_g,
           mix_pre_g, w_in, q_norm_g, w_uq, kv_norm_g, w_ukv, sink, w_o_a, w_o_b, w_out,
           mix_post_g, ffn2_pre_g, ffn2_w_in, ffn2_w_out, ffn2_post_g):
    assert ffn1_w_in.shape[0] == 1, "single layer"
    bf16 = jnp.bfloat16
    f1 = (ffn1_pre_g, ffn1_w_in[0].astype(bf16), ffn1_w_out[0].astype(bf16), ffn1_post_g)
    f2 = (ffn2_pre_g, ffn2_w_in[0].astype(bf16), ffn2_w_out[0].astype(bf16), ffn2_post_g)
    w_proj, w_gate, wq, wk, wvt, w_vst = _pack_weights(w_in[0], w_uq[0], w_ukv[0])
    woa, wob, wout = w_o_a[0].astype(bf16), w_o_b[0].astype(bf16), w_out[0].astype(bf16)
    sink = sink[0]

    def front(x, tabs):
        tm = _row_tile(x.shape[1])
        x1 = _ffn(x, *f1, tm)
        return x1, _proj(x1, tabs, mix_pre_g, w_proj, w_vst, q_norm_g, wq, kv_norm_g, wk, wvt,
                         tm)

    lead = jnp.concatenate([jnp.zeros((LEAD_PAD, D_MODEL), x_prompt.dtype),
                            meta_tokens.astype(x_prompt.dtype)], axis=0)[None]
    _, (_, ka_lead, va_lead, _, kw_lead, vw_lead) = front(
        lead, _rope_tables(jnp.arange(BLOCK) - LEAD_PAD))
    seq_tabs = _rope_tables(jnp.arange(max(x_prompt.shape[1], x_sample.shape[1])) + N_META)

    def trunk(x):
        s = x.shape[1]
        tm = _row_tile(s)
        x1, (qa, ka, va, qw, kw, vw) = front(x, seq_tabs)
        o_a = _mla(qa, ka, va, ka_lead, va_lead, tq=MXU_COLS, ck=1024)
        o_b = _window(sink, qw, kw, vw, kw_lead, vw_lead, tq=8 * BLOCK)
        x2 = _mixout(x1, o_a, o_b, mix_pre_g, w_gate, woa, wob, wout, mix_post_g, tm)
        return _ffn(x2, *f2, tm)

    return (trunk(x_prompt), trunk(x_sample))
```

```python
import functools

import jax
import jax.numpy as jnp
from jax import lax
from jax.experimental import pallas as pl
from jax.experimental.pallas import tpu as pltpu

D_MODEL = 1024
N_META = 16
BLOCK = 128
WINDOW = 128
assert WINDOW == BLOCK, "the band masks below are written for a window of one block"
LEAD_PAD = BLOCK - N_META
ROPE_THETA = 10000.0
EPS = 1e-6
NEG = -1e30
MLA_HEADS = 8
MLA_Q_LORA = 384
MLA_KV_LORA = 256
MLA_NOPE = 64
MLA_ROPE = 32
MLA_V = 64
SWA_HEADS = 8
SWA_KV_HEADS = 2
SWA_HEAD_DIM = 64
SWA_GROUP = SWA_HEADS // SWA_KV_HEADS
D_FF = 2816

LANES = 128
ROPE_PARTNER = LANES // 2
MLA_SCALE = (MLA_NOPE + MLA_ROPE) ** -0.5
SWA_SCALE = SWA_HEAD_DIM ** -0.5
V7X_VMEM_BYTES = 64 * 1024 * 1024
VMEM_LIMIT = (V7X_VMEM_BYTES * 7) // 8

_C_Q = (0, MLA_Q_LORA)
_C_KV = (_C_Q[1], _C_Q[1] + MLA_KV_LORA)
_C_QS = (_C_KV[1], _C_KV[1] + SWA_HEADS * SWA_HEAD_DIM)
_C_KS = (_C_QS[1], _C_QS[1] + LANES)
_C_KR = (_C_KS[1], _C_KS[1] + LANES)

MXU_COLS = 256
FFN_CHUNKS = ((0, 6 * MXU_COLS), (6 * MXU_COLS, D_FF))
assert all((hi - lo) % MXU_COLS == 0 for lo, hi in FFN_CHUNKS)


def _const_spec(shape):
    zeros = (0,) * len(shape)
    return pl.BlockSpec(shape, lambda *_: zeros, pipeline_mode=pl.Buffered(1))


def _params(n_axes):
    return pltpu.CompilerParams(
        dimension_semantics=("parallel",) * n_axes, vmem_limit_bytes=VMEM_LIMIT)


def _rms(x, g):
    y = x * lax.rsqrt(jnp.mean(x * x, axis=-1, keepdims=True) + EPS)
    return y * g


def _dot(a, b):
    return jnp.dot(a, b, preferred_element_type=jnp.float32)


def _dot_nt(a, b):
    return lax.dot_general(a, b, (((1,), (1,)), ((), ())),
                           preferred_element_type=jnp.float32)


def _swiglu_residual(x, pre_g, w_in_ref, w_out_ref, post_g):
    h = _rms(x, pre_g).astype(jnp.bfloat16)
    acc = None
    for lo, hi in FFN_CHUNKS:
        g = _dot(h, w_in_ref[:, lo:hi])
        u = _dot(h, w_in_ref[:, D_FF + lo:D_FF + hi])
        a = (g * jax.nn.sigmoid(g) * u).astype(jnp.bfloat16)
        part = _dot(a, w_out_ref[lo:hi, :])
        acc = part if acc is None else acc + part
    return x + 0.5 * _rms(acc, post_g)


def _ffn_kernel(x_ref, pre_g_ref, w_in_ref, w_out_ref, post_g_ref, o_ref):
    half = x_ref.shape[1] // 2
    for rows in (slice(0, half), slice(half, 2 * half)):
        o_ref[0, rows, :] = _swiglu_residual(x_ref[0, rows, :], pre_g_ref[...], w_in_ref,
                                             w_out_ref, post_g_ref[...])


def _ffn(x, pre_g, w_in, w_out, post_g, tm):
    b, s, d = x.shape
    row = pl.BlockSpec((1, tm, d), lambda i, t: (i, t, 0))
    return pl.pallas_call(
        _ffn_kernel,
        grid=(b, s // tm),
        in_specs=[row, _const_spec(pre_g.shape), _const_spec(w_in.shape),
                  _const_spec(w_out.shape), _const_spec(post_g.shape)],
        out_specs=row,
        out_shape=jax.ShapeDtypeStruct(x.shape, jnp.float32),
        compiler_params=_params(2),
        name="ffn",
    )(x, pre_g, w_in, w_out, post_g)


def _proj_kernel(x_ref, ck_ref, sk_ref, cw_ref, sw_ref, pre_g_ref, w_ref, wvst_ref, qn_g_ref,
                 wq_ref, kvn_g_ref, wk_ref, wvt_ref,
                 qa_ref, ka_ref, vt_ref, qw_ref, kw_ref, vw_ref):
    bf16 = jnp.bfloat16
    h = _rms(x_ref[0], pre_g_ref[...]).astype(bf16)
    ck_t, sk_t, cw_t, sw_t = ck_ref[...], sk_ref[...], cw_ref[...], sw_ref[...]
    lane = lax.broadcasted_iota(jnp.int32, (1, LANES), 1)

    proj = _dot(h, w_ref[...])

    def cols(rng):
        return proj[:, rng[0]:rng[1]]

    def rope(x, cos, sin_signed):
        return x * cos + pltpu.roll(x, ROPE_PARTNER, 1) * sin_signed

    qn = _rms(cols(_C_Q), qn_g_ref[...]).astype(bf16)
    qq = _dot(qn, wq_ref[...])
    for hd in range(MLA_HEADS):
        q_rope = rope(qq[:, hd * LANES:(hd + 1) * LANES], ck_t, sk_t)
        qa_ref[0, hd] = (q_rope * (MLA_SCALE * LOG2_E)).astype(bf16)

    kvn = _rms(cols(_C_KV), kvn_g_ref[...]).astype(bf16)
    kk = _dot(kvn, wk_ref[...])
    k_rope = rope(cols(_C_KR), ck_t, sk_t)
    for hd in range(MLA_HEADS):
        ka_ref[0, hd] = (kk[:, hd * LANES:(hd + 1) * LANES] + k_rope).astype(bf16)

    vt = _dot_nt(wvt_ref[...], kvn)
    is_v_row = lax.broadcasted_iota(jnp.int32, (LANES, 1), 0) < MLA_V
    for hd in range(MLA_HEADS):
        vt_ref[0, hd] = jnp.where(is_v_row, vt[hd * LANES:(hd + 1) * LANES], 1.0).astype(bf16)

    quarter = SWA_HEAD_DIM // 2
    even_lanes = lane % SWA_HEAD_DIM < quarter
    qs = cols(_C_QS)
    for pair in range(SWA_HEADS // 2):
        q_pair = rope(qs[:, pair * LANES:(pair + 1) * LANES], cw_t, sw_t) * (SWA_SCALE * LOG2_E)
        qw_ref[0, 2 * pair] = jnp.where(even_lanes, q_pair, 0.0).astype(bf16)
        qw_ref[0, 2 * pair + 1] = jnp.where(even_lanes, 0.0, q_pair).astype(bf16)

    k_pair = rope(cols(_C_KS), cw_t, sw_t)
    from_below = pltpu.roll(k_pair, quarter, 1)
    from_above = pltpu.roll(k_pair, LANES - quarter, 1)
    kw_ref[0, 0] = jnp.where(even_lanes, k_pair, from_below).astype(bf16)
    kw_ref[0, 1] = jnp.where(even_lanes, from_above, k_pair).astype(bf16)

    vtw = _dot_nt(wvst_ref[...], h)
    ones = jnp.ones((SWA_HEAD_DIM, vtw.shape[1]), jnp.float32)
    for g in range(SWA_KV_HEADS):
        v_g = vtw[g * SWA_HEAD_DIM:(g + 1) * SWA_HEAD_DIM]
        vw_ref[0, g] = jnp.concatenate([v_g, ones], axis=0).astype(bf16)


def _proj(x1, tabs, pre_g, w_proj, w_vst, qn_g, wq, kvn_g, wk, wvt, tm):
    b, s, d = x1.shape
    row = pl.BlockSpec((1, tm, d), lambda i, t: (i, t, 0))
    tab_spec = pl.BlockSpec((tm, LANES), lambda i, t: (t, 0))

    def heads(n):
        return (pl.BlockSpec((1, n, tm, LANES), lambda i, t: (i, 0, t, 0)),
                jax.ShapeDtypeStruct((b, n, s, LANES), jnp.bfloat16))

    def heads_t(n):
        return (pl.BlockSpec((1, n, LANES, tm), lambda i, t: (i, 0, 0, t)),
                jax.ShapeDtypeStruct((b, n, LANES, s), jnp.bfloat16))

    outs = (heads(MLA_HEADS), heads(MLA_HEADS), heads_t(MLA_HEADS),
            heads(SWA_HEADS), heads(SWA_KV_HEADS), heads_t(SWA_KV_HEADS))
    consts = (pre_g, w_proj, w_vst, qn_g, wq, kvn_g, wk, wvt)
    return pl.pallas_call(
        _proj_kernel,
        grid=(b, s // tm),
        in_specs=[row] + [tab_spec] * len(tabs) + [_const_spec(c.shape) for c in consts],
        out_specs=[o[0] for o in outs],
        out_shape=[o[1] for o in outs],
        compiler_params=_params(2),
        name="proj",
    )(x1, *tabs, *consts)


LOG2_E = 1.4426950408889634


def _mla_kernel(q_ref, k_ref, vt_ref, kl_ref, vtl_ref, o_ref, s_a, s_b, lead_a, lead_b, ot_scr,
                *, tq, ck):
    seq = k_ref.shape[2]
    n_chunks = seq // ck
    n_tiles = seq // tq
    sublanes = 8
    lead_ok = lax.broadcasted_iota(jnp.int32, (BLOCK, 1), 0) >= LEAD_PAD

    def fold_max(s):
        return jnp.max(s.reshape(s.shape[0] // sublanes, sublanes, tq), axis=0)

    def rows(t):
        return pl.ds(pl.multiple_of(t * tq, tq), tq)

    def score_chunk(j, q, c, s_new, mx):
        s = _dot_nt(k_ref[0, j, c * ck:(c + 1) * ck, :], q)
        s_new[c] = s
        return jnp.maximum(mx, fold_max(s))

    def value_chunk(j, c, s_old, m, acc):
        p = jnp.exp2(s_old[c] - m).astype(jnp.bfloat16)
        return acc + _dot(vt_ref[0, j, :, c * ck:(c + 1) * ck], p)

    def emit(j, t, acc):
        o_t = acc[:MLA_V] * (1.0 / acc[MLA_V:MLA_V + 1])
        if j % 2 == 0:
            ot_scr[t] = o_t
        else:
            both = jnp.concatenate([ot_scr[t], o_t], axis=0)
            slab = slice((j // 2) * LANES, (j // 2 + 1) * LANES)
            o_ref[0, rows(t), slab] = both.T.astype(o_ref.dtype)

    def lead_values(j, lead_old, mx_old):
        m = jnp.max(mx_old, axis=0, keepdims=True)
        return m, _dot(vtl_ref[0, j], jnp.exp2(lead_old[...] - m).astype(jnp.bfloat16))

    def stage(new, old, older, buf_new, buf_old, mx_old, acc_older):
        if older is not None:
            emit(*older, acc_older)
        j, t = new
        s_new, lead_new = buf_new
        q = q_ref[0, j, rows(t), :]
        s_lead = jnp.where(lead_ok, _dot_nt(kl_ref[0, j], q), NEG)
        lead_new[...] = s_lead
        mx, acc = fold_max(s_lead), None
        if old is not None:
            m, acc = lead_values(old[0], buf_old[1], mx_old)
        for c in range(n_chunks):
            mx = score_chunk(j, q, c, s_new, mx)
            if old is not None:
                acc = value_chunk(old[0], c, buf_old[0], m, acc)
        return mx, acc

    buf_a, buf_b = (s_a, lead_a), (s_b, lead_b)
    n_heads = q_ref.shape[1]
    end = n_tiles - 1
    for j in range(n_heads):
        if j == 0:
            mx, _ = stage((0, 0), None, None, buf_a, None, None, None)
            mx, acc = stage((0, 1), (0, 0), None, buf_b, buf_a, mx, None)
        else:
            mx, acc = stage((j, 0), (j - 1, end), (j - 1, end - 1), buf_a, buf_b, mx, acc)
            mx, acc = stage((j, 1), (j, 0), (j - 1, end), buf_b, buf_a, mx, acc)

        def pair(i, carry, j=j):
            t = 2 * i + 2
            mx, acc = stage((j, t), (j, t - 1), (j, t - 2), buf_a, buf_b, *carry)
            return stage((j, t + 1), (j, t), (j, t - 1), buf_b, buf_a, mx, acc)

        mx, acc = lax.fori_loop(0, n_tiles // 2 - 1, pair, (mx, acc))
    last = n_heads - 1
    emit(last, end - 1, acc)
    m, acc = lead_values(last, lead_b, mx)
    for c in range(n_chunks):
        acc = value_chunk(last, c, s_b, m, acc)
    emit(last, end, acc)


def _mla_heads_per_step(nh, s, tq, ck):
    bf16_bytes, f32_bytes, pipeline_buffers = 2, 4, 2
    scratch = 2 * (s + BLOCK) * tq * f32_bytes + s * MLA_V * f32_bytes
    in_flight = 2 * ck * tq * f32_bytes
    for hp in range(nh, 0, -2):
        per_head = (3 * s * LANES + s * MLA_V) * bf16_bytes
        if nh % hp == 0 and pipeline_buffers * hp * per_head + scratch + in_flight <= VMEM_LIMIT:
            return hp
    raise ValueError("sequence too long for the resident-key MLA kernel")


def _mla(q, k, vt, k_lead, vt_lead, tq, ck):
    b, nh, s, _ = q.shape
    hp = _mla_heads_per_step(nh, s, tq, ck)
    assert s % ck == 0 and s % (2 * tq) == 0
    seq_spec = pl.BlockSpec((1, hp, s, LANES), lambda i, p: (i, p, 0, 0))
    vt_spec = pl.BlockSpec((1, hp, LANES, s), lambda i, p: (i, p, 0, 0))
    lead_spec = pl.BlockSpec((1, hp, BLOCK, LANES), lambda i, p: (0, p, 0, 0))
    score_buf = pltpu.VMEM((s // ck, ck, tq), jnp.float32)
    lead_buf = pltpu.VMEM((BLOCK, tq), jnp.float32)
    return pl.pallas_call(
        functools.partial(_mla_kernel, tq=tq, ck=ck),
        grid=(b, nh // hp),
        in_specs=[seq_spec, seq_spec, vt_spec, lead_spec, lead_spec],
        out_specs=pl.BlockSpec((1, s, hp * MLA_V), lambda i, p: (i, 0, p)),
        out_shape=jax.ShapeDtypeStruct((b, s, nh * MLA_V), jnp.bfloat16),
        scratch_shapes=[score_buf, score_buf, lead_buf, lead_buf,
                        pltpu.VMEM((s // tq, MLA_V, tq), jnp.float32)],
        compiler_params=_params(2),
        name="mla",
    )(q, k, vt, k_lead, vt_lead)


def _window_kernel(sink_ref, q_ref, kl_ref, kc_ref, kr_ref, km_ref,
                   vl_ref, vc_ref, vr_ref, vm_ref, o_ref):
    tq = q_ref.shape[2]
    n_blk = tq // BLOCK
    t = pl.program_id(1)
    n_t = pl.num_programs(1)
    n_keys = 4 * BLOCK
    n_cols = SWA_GROUP * BLOCK
    key = lax.broadcasted_iota(jnp.int32, (n_keys, n_cols), 0)
    query = lax.broadcasted_iota(jnp.int32, (n_keys, n_cols), 1) % BLOCK
    in_left = jnp.logical_and(key < BLOCK, key >= query)
    in_right = jnp.logical_and(jnp.logical_and(key >= 2 * BLOCK, key < 3 * BLOCK),
                               key - 2 * BLOCK <= query)
    in_rest = jnp.logical_or(jnp.logical_and(key >= BLOCK, key < 2 * BLOCK),
                             key >= 3 * BLOCK + LEAD_PAD)
    sublanes = 8

    def keys_of(blk, left_ref, center_ref, right_ref, lead_ref, g):
        first, last = blk == 0, blk == n_blk - 1
        return jnp.concatenate(
            [left_ref[0, g] if first else center_ref[0, g, (blk - 1) * BLOCK:blk * BLOCK, :],
             center_ref[0, g, blk * BLOCK:(blk + 1) * BLOCK, :],
             right_ref[0, g] if last else center_ref[0, g, (blk + 1) * BLOCK:(blk + 2) * BLOCK, :],
             lead_ref[0, g]], axis=0)

    def values_t_of(blk, g):
        first, last = blk == 0, blk == n_blk - 1
        return jnp.concatenate(
            [vl_ref[0, g] if first else vc_ref[0, g, :, (blk - 1) * BLOCK:blk * BLOCK],
             vc_ref[0, g, :, blk * BLOCK:(blk + 1) * BLOCK],
             vr_ref[0, g] if last else vc_ref[0, g, :, (blk + 1) * BLOCK:(blk + 2) * BLOCK],
             vm_ref[0, g]], axis=1)

    def scores(g, blk):
        ok_left = jnp.logical_and(in_left, t > 0) if blk == 0 else in_left
        ok_right = jnp.logical_and(in_right, t < n_t - 1) if blk == n_blk - 1 else in_right
        ok = jnp.logical_or(jnp.logical_or(ok_left, ok_right), in_rest)
        q = q_ref[0, g * SWA_GROUP:(g + 1) * SWA_GROUP, blk * BLOCK:(blk + 1) * BLOCK, :]
        s = _dot_nt(keys_of(blk, kl_ref, kc_ref, kr_ref, km_ref, g),
                    q.reshape(n_cols, LANES))
        return jnp.where(ok, s, NEG)

    def finish(g, blk, s):
        sink = jnp.concatenate(
            [jnp.full((1, BLOCK), sink_ref[g * SWA_GROUP + r] * LOG2_E, jnp.float32)
             for r in range(SWA_GROUP)], axis=1)
        folded = jnp.max(s.reshape(n_keys // sublanes, sublanes, n_cols), axis=0)
        m = jnp.maximum(jnp.max(folded, axis=0, keepdims=True), sink)
        p = jnp.exp2(s - m).astype(jnp.bfloat16)
        acc = _dot(values_t_of(blk, g), p)
        o_t = acc[:SWA_HEAD_DIM] * (
            1.0 / (acc[SWA_HEAD_DIM:SWA_HEAD_DIM + 1] + jnp.exp2(sink - m)))
        for pair in range(SWA_GROUP // 2):
            both = jnp.concatenate(
                [o_t[:, (2 * pair) * BLOCK:(2 * pair + 1) * BLOCK],
                 o_t[:, (2 * pair + 1) * BLOCK:(2 * pair + 2) * BLOCK]], axis=0)
            slab = g * (SWA_GROUP // 2) + pair
            o_ref[0, blk * BLOCK:(blk + 1) * BLOCK, slab * LANES:(slab + 1) * LANES] = (
                both.T.astype(o_ref.dtype))

    units = [(g, blk) for g in range(SWA_KV_HEADS) for blk in range(n_blk)]
    ahead = 3
    pending = [scores(*u) for u in units[:ahead]]
    for i, unit in enumerate(units):
        if i + ahead < len(units):
            pending.append(scores(*units[i + ahead]))
        finish(*unit, pending.pop(0))


def _window(sink, q, k, v, k_lead, v_lead, tq):
    b, nh, s, _ = q.shape
    n_blk = s // BLOCK
    per = tq // BLOCK

    def specs(n, transposed):
        def spec(rows, index):
            if transposed:
                return pl.BlockSpec((1, n, LANES, rows), lambda i, t: (*index(i, t)[:2], 0,
                                                                       index(i, t)[2]))
            return pl.BlockSpec((1, n, rows, LANES), lambda i, t: (*index(i, t), 0))

        return [spec(BLOCK, lambda i, t: (i, 0, jnp.maximum(t * per - 1, 0))),
                spec(tq, lambda i, t: (i, 0, t)),
                spec(BLOCK, lambda i, t: (i, 0, jnp.minimum((t + 1) * per, n_blk - 1))),
                spec(BLOCK, lambda i, t: (0, 0, 0))]

    smem = pl.BlockSpec(memory_space=pltpu.SMEM)
    return pl.pallas_call(
        _window_kernel,
        grid=(b, s // tq),
        in_specs=([smem, specs(nh, False)[1]] + specs(SWA_KV_HEADS, False)
                  + specs(SWA_KV_HEADS, True)),
        out_specs=pl.BlockSpec((1, tq, nh * SWA_HEAD_DIM), lambda i, t: (i, t, 0)),
        out_shape=jax.ShapeDtypeStruct((b, s, nh * SWA_HEAD_DIM), jnp.bfloat16),
        compiler_params=_params(2),
        name="window",
    )(sink, q, k, k, k, k_lead, v, v, v, v_lead)


def _mixout_kernel(x_ref, oa_ref, ob_ref, pre_g_ref, wg_ref, woa_ref, wob_ref, wout_ref,
                   post_g_ref, o_ref):
    half = x_ref.shape[1] // 2
    for rows in (slice(0, half), slice(half, 2 * half)):
        x1 = x_ref[0, rows, :]
        h = _rms(x1, pre_g_ref[...]).astype(jnp.bfloat16)
        y_a = _dot(oa_ref[0, rows, :], woa_ref[...])
        y_b = _dot(ob_ref[0, rows, :], wob_ref[...])
        merged = (jax.nn.sigmoid(_dot(h, wg_ref[:, :D_MODEL])) * y_a
                  + jax.nn.sigmoid(_dot(h, wg_ref[:, D_MODEL:])) * y_b)
        o_ref[0, rows, :] = x1 + _rms(_dot(merged.astype(jnp.bfloat16), wout_ref[...]),
                                      post_g_ref[...])


def _mixout(x1, o_a, o_b, pre_g, wg, woa, wob, wout, post_g, tm):
    b, s, d = x1.shape
    row = pl.BlockSpec((1, tm, d), lambda i, t: (i, t, 0))
    half = pl.BlockSpec((1, tm, o_a.shape[2]), lambda i, t: (i, t, 0))
    return pl.pallas_call(
        _mixout_kernel,
        grid=(b, s // tm),
        in_specs=[row, half, half, _const_spec(pre_g.shape), _const_spec(wg.shape),
                  _const_spec(woa.shape), _const_spec(wob.shape), _const_spec(wout.shape),
                  _const_spec(post_g.shape)],
        out_specs=row,
        out_shape=jax.ShapeDtypeStruct(x1.shape, jnp.float32),
        compiler_params=_params(2),
        name="mixout",
    )(x1, o_a, o_b, pre_g, wg, woa, wob, wout, post_g)


def _pack_weights(w_in, w_uq, w_ukv):
    bf16 = jnp.bfloat16
    d = w_in.shape[0]
    o = 0
    parts = []
    for n in (MLA_Q_LORA, MLA_KV_LORA, MLA_ROPE, SWA_HEADS * SWA_HEAD_DIM,
              SWA_KV_HEADS * SWA_HEAD_DIM, SWA_KV_HEADS * SWA_HEAD_DIM, D_MODEL, D_MODEL):
        parts.append(w_in[:, o:o + n])
        o += n
    w_cq, w_ckv, w_kr, w_qs, w_ks, w_vs, w_ga, w_gb = parts

    def mla_slab(nope, rope):
        half = MLA_ROPE // 2
        split = ROPE_PARTNER - half
        tail = jnp.zeros(nope.shape[:-1] + (LANES - MLA_NOPE - MLA_ROPE,), nope.dtype)
        return jnp.concatenate([rope[..., :half], nope[..., :split], rope[..., half:],
                                nope[..., split:], tail], axis=-1)

    def window_pairs(w):
        quarter = SWA_HEAD_DIM // 2
        w = w.reshape(d, -1, 2, 2, quarter)
        return jnp.swapaxes(w, 2, 3).reshape(d, -1)

    kr_slab = mla_slab(jnp.zeros((d, MLA_NOPE), w_kr.dtype), w_kr)
    w_proj = jnp.concatenate([w_cq, w_ckv, window_pairs(w_qs), window_pairs(w_ks), kr_slab],
                             axis=1).astype(bf16)
    w_gate = jnp.concatenate([w_ga, w_gb], axis=1).astype(bf16)
    w_vst = w_vs.T.astype(bf16)

    r = w_uq.shape[0]
    uq = w_uq.reshape(r, MLA_HEADS, MLA_NOPE + MLA_ROPE)
    wq = mla_slab(uq[..., :MLA_NOPE], uq[..., MLA_NOPE:]).reshape(r, -1).astype(bf16)

    r = w_ukv.shape[0]
    ukv = w_ukv.reshape(r, MLA_HEADS, MLA_NOPE + MLA_V)
    no_rope = jnp.zeros((r, MLA_HEADS, MLA_ROPE), w_ukv.dtype)
    wk = mla_slab(ukv[..., :MLA_NOPE], no_rope).reshape(r, -1).astype(bf16)
    zeros = jnp.zeros((r, MLA_HEADS, LANES - MLA_V), w_ukv.dtype)
    v_slab = jnp.concatenate([ukv[..., MLA_NOPE:], zeros], axis=-1)
    wvt = v_slab.reshape(r, -1).T.astype(bf16)
    return w_proj, w_gate, wq, wk, wvt, w_vst


def _rope_tables(pos):
    pos = pos.astype(jnp.float32)[:, None]
    lane = jnp.arange(LANES)[None, :]
    first = lane < ROPE_PARTNER

    def inv_freq(half):
        return ROPE_THETA ** (-jnp.arange(half, dtype=jnp.float32) / half)

    half_k = MLA_ROPE // 2
    ang_k = pos * inv_freq(half_k)[lane % half_k]
    is_rope = lane % ROPE_PARTNER < half_k
    ck = jnp.where(is_rope, jnp.cos(ang_k), 1.0)
    sk = jnp.where(is_rope, jnp.where(first, -jnp.sin(ang_k), jnp.sin(ang_k)), 0.0)
    half_w = SWA_HEAD_DIM // 2
    ang_w = pos * inv_freq(half_w)[lane % half_w]
    return ck, sk, jnp.cos(ang_w), jnp.where(first, -jnp.sin(ang_w), jnp.sin(ang_w))


def _row_tile(s):
    return min(s, 1024)


def kernel(x_prompt, x_sample, meta_tokens, ffn1_pre_g, ffn1_w_in, ffn1_w_out, ffn1_post_g,
           mix_pre_g, w_in, q_norm_g, w_uq, kv_norm_g, w_ukv, sink, w_o_a, w_o_b, w_out,
           mix_post_g, ffn2_pre_g, ffn2_w_in, ffn2_w_out, ffn2_post_g):
    assert ffn1_w_in.shape[0] == 1, "single layer"
    bf16 = jnp.bfloat16
    f1 = (ffn1_pre_g, ffn1_w_in[0].astype(bf16), ffn1_w_out[0].astype(bf16), ffn1_post_g)
    f2 = (ffn2_pre_g, ffn2_w_in[0].astype(bf16), ffn2_w_out[0].astype(bf16), ffn2_post_g)
    w_proj, w_gate, wq, wk, wvt, w_vst = _pack_weights(w_in[0], w_uq[0], w_ukv[0])
    woa, wob, wout = w_o_a[0].astype(bf16), w_o_b[0].astype(bf16), w_out[0].astype(bf16)
    sink = sink[0]

    def front(x, tabs):
        tm = _row_tile(x.shape[1])
        x1 = _ffn(x, *f1, tm)
        return x1, _proj(x1, tabs, mix_pre_g, w_proj, w_vst, q_norm_g, wq, kv_norm_g, wk, wvt,
                         tm)

    lead = jnp.concatenate([jnp.zeros((LEAD_PAD, D_MODEL), x_prompt.dtype),
                            meta_tokens.astype(x_prompt.dtype)], axis=0)[None]
    _, (_, ka_lead, va_lead, _, kw_lead, vw_lead) = front(
        lead, _rope_tables(jnp.arange(BLOCK) - LEAD_PAD))
    seq_tabs = _rope_tables(jnp.arange(max(x_prompt.shape[1], x_sample.shape[1])) + N_META)

    def trunk(x):
        s = x.shape[1]
        tm = _row_tile(s)
        x1, (qa, ka, va, qw, kw, vw) = front(x, seq_tabs)
        o_a = _mla(qa, ka, va, ka_lead, va_lead, tq=MXU_COLS, ck=1024)
        o_b = _window(sink, qw, kw, vw, kw_lead, vw_lead, tq=8 * BLOCK)
        x2 = _mixout(x1, o_a, o_b, mix_pre_g, w_gate, woa, wob, wout, mix_post_g, tm // 2)
        return _ffn(x2, *f2, tm)

    return (trunk(x_prompt), trunk(x_sample))
```

```python
import functools

import jax
import jax.numpy as jnp
from jax import lax
from jax.experimental import pallas as pl
from jax.experimental.pallas import tpu as pltpu

D_MODEL = 1024
N_META = 16
BLOCK = 128
WINDOW = 128
assert WINDOW == BLOCK, "the band masks below are written for a window of one block"
LEAD_PAD = BLOCK - N_META
ROPE_THETA = 10000.0
EPS = 1e-6
NEG = -1e30
MLA_HEADS = 8
MLA_Q_LORA = 384
MLA_KV_LORA = 256
MLA_NOPE = 64
MLA_ROPE = 32
MLA_V = 64
SWA_HEADS = 8
SWA_KV_HEADS = 2
SWA_HEAD_DIM = 64
SWA_GROUP = SWA_HEADS // SWA_KV_HEADS
D_FF = 2816

LANES = 128
ROPE_PARTNER = LANES // 2
assert SWA_HEAD_DIM == 2 * MLA_ROPE, "MLA rope frequencies must be every second window one"
MLA_SCALE = (MLA_NOPE + MLA_ROPE) ** -0.5
SWA_SCALE = SWA_HEAD_DIM ** -0.5
V7X_VMEM_BYTES = 64 * 1024 * 1024
VMEM_LIMIT = (V7X_VMEM_BYTES * 7) // 8

_C_Q = (0, MLA_Q_LORA)
_C_KV = (_C_Q[1], _C_Q[1] + MLA_KV_LORA)
_C_QS = (_C_KV[1], _C_KV[1] + SWA_HEADS * SWA_HEAD_DIM)
_C_KS = (_C_QS[1], _C_QS[1] + LANES)
_C_KR = (_C_KS[1], _C_KS[1] + LANES)

MXU_COLS = 256
FFN_CHUNKS = ((0, 6 * MXU_COLS), (6 * MXU_COLS, D_FF))
assert all((hi - lo) % MXU_COLS == 0 for lo, hi in FFN_CHUNKS)


def _const_spec(shape):
    zeros = (0,) * len(shape)
    return pl.BlockSpec(shape, lambda *_: zeros, pipeline_mode=pl.Buffered(1))


def _params(n_axes):
    return pltpu.CompilerParams(
        dimension_semantics=("parallel",) * n_axes, vmem_limit_bytes=VMEM_LIMIT)


def _rms(x, g):
    y = x * lax.rsqrt(jnp.mean(x * x, axis=-1, keepdims=True) + EPS)
    return y * g


def _dot(a, b):
    return jnp.dot(a, b, preferred_element_type=jnp.float32)


def _dot_nt(a, b):
    return lax.dot_general(a, b, (((1,), (1,)), ((), ())),
                           preferred_element_type=jnp.float32)


def _swiglu_residual(x, pre_g, w_in_ref, w_out_ref, post_g):
    h = _rms(x, pre_g).astype(jnp.bfloat16)
    acc = None
    for lo, hi in FFN_CHUNKS:
        g = _dot(h, w_in_ref[:, lo:hi])
        u = _dot(h, w_in_ref[:, D_FF + lo:D_FF + hi])
        a = (g * jax.nn.sigmoid(g) * u).astype(jnp.bfloat16)
        part = _dot(a, w_out_ref[lo:hi, :])
        acc = part if acc is None else acc + part
    return x + 0.5 * _rms(acc, post_g)


def _ffn_kernel(x_ref, pre_g_ref, w_in_ref, w_out_ref, post_g_ref, o_ref):
    half = x_ref.shape[1] // 2
    for rows in (slice(0, half), slice(half, 2 * half)):
        o_ref[0, rows, :] = _swiglu_residual(x_ref[0, rows, :], pre_g_ref[...], w_in_ref,
                                             w_out_ref, post_g_ref[...])


def _ffn(x, pre_g, w_in, w_out, post_g, tm):
    b, s, d = x.shape
    row = pl.BlockSpec((1, tm, d), lambda i, t: (i, t, 0))
    return pl.pallas_call(
        _ffn_kernel,
        grid=(b, s // tm),
        in_specs=[row, _const_spec(pre_g.shape), _const_spec(w_in.shape),
                  _const_spec(w_out.shape), _const_spec(post_g.shape)],
        out_specs=row,
        out_shape=jax.ShapeDtypeStruct(x.shape, jnp.float32),
        compiler_params=_params(2),
        name="ffn",
    )(x, pre_g, w_in, w_out, post_g)


def _proj_kernel(x_ref, cw_ref, sw_ref, pre_g_ref, w_ref, wvst_ref, qn_g_ref,
                 wq_ref, kvn_g_ref, wk_ref, wvt_ref,
                 qa_ref, ka_ref, vt_ref, qw_ref, kw_ref, vw_ref):
    bf16 = jnp.bfloat16
    h = _rms(x_ref[0], pre_g_ref[...]).astype(bf16)
    cw_t, sw_t = cw_ref[...], sw_ref[...]
    lane = lax.broadcasted_iota(jnp.int32, (1, LANES), 1)
    mla_rope_lane = jnp.logical_and(lane % ROPE_PARTNER < MLA_ROPE, lane % 2 == 0)
    ck_t = jnp.where(mla_rope_lane, cw_t, 1.0)
    sk_t = jnp.where(mla_rope_lane, sw_t, 0.0)

    proj = _dot(h, w_ref[...])

    def cols(rng):
        return proj[:, rng[0]:rng[1]]

    def rope(x, cos, sin_signed):
        return x * cos + pltpu.roll(x, ROPE_PARTNER, 1) * sin_signed

    qn = _rms(cols(_C_Q), qn_g_ref[...]).astype(bf16)
    qq = _dot(qn, wq_ref[...])
    for hd in range(MLA_HEADS):
        q_rope = rope(qq[:, hd * LANES:(hd + 1) * LANES], ck_t, sk_t)
        qa_ref[0, hd] = (q_rope * (MLA_SCALE * LOG2_E)).astype(bf16)

    kvn = _rms(cols(_C_KV), kvn_g_ref[...]).astype(bf16)
    kk = _dot(kvn, wk_ref[...])
    k_rope = rope(cols(_C_KR), ck_t, sk_t)
    for hd in range(MLA_HEADS):
        ka_ref[0, hd] = (kk[:, hd * LANES:(hd + 1) * LANES] + k_rope).astype(bf16)

    vt = _dot_nt(wvt_ref[...], kvn)
    is_v_row = lax.broadcasted_iota(jnp.int32, (LANES, 1), 0) < MLA_V
    for hd in range(MLA_HEADS):
        vt_ref[0, hd] = jnp.where(is_v_row, vt[hd * LANES:(hd + 1) * LANES], 1.0).astype(bf16)

    quarter = SWA_HEAD_DIM // 2
    even_lanes = lane % SWA_HEAD_DIM < quarter
    qs = cols(_C_QS)
    for pair in range(SWA_HEADS // 2):
        q_pair = rope(qs[:, pair * LANES:(pair + 1) * LANES], cw_t, sw_t) * (SWA_SCALE * LOG2_E)
        qw_ref[0, 2 * pair] = jnp.where(even_lanes, q_pair, 0.0).astype(bf16)
        qw_ref[0, 2 * pair + 1] = jnp.where(even_lanes, 0.0, q_pair).astype(bf16)

    k_pair = rope(cols(_C_KS), cw_t, sw_t)
    from_below = pltpu.roll(k_pair, quarter, 1)
    from_above = pltpu.roll(k_pair, LANES - quarter, 1)
    kw_ref[0, 0] = jnp.where(even_lanes, k_pair, from_below).astype(bf16)
    kw_ref[0, 1] = jnp.where(even_lanes, from_above, k_pair).astype(bf16)

    vtw = _dot_nt(wvst_ref[...], h)
    ones = jnp.ones((SWA_HEAD_DIM, vtw.shape[1]), jnp.float32)
    for g in range(SWA_KV_HEADS):
        v_g = vtw[g * SWA_HEAD_DIM:(g + 1) * SWA_HEAD_DIM]
        vw_ref[0, g] = jnp.concatenate([v_g, ones], axis=0).astype(bf16)


def _proj(x1, tabs, pre_g, w_proj, w_vst, qn_g, wq, kvn_g, wk, wvt, tm):
    b, s, d = x1.shape
    row = pl.BlockSpec((1, tm, d), lambda i, t: (i, t, 0))
    tab_spec = pl.BlockSpec((tm, LANES), lambda i, t: (t, 0))

    def heads(n):
        return (pl.BlockSpec((1, n, tm, LANES), lambda i, t: (i, 0, t, 0)),
                jax.ShapeDtypeStruct((b, n, s, LANES), jnp.bfloat16))

    def heads_t(n):
        return (pl.BlockSpec((1, n, LANES, tm), lambda i, t: (i, 0, 0, t)),
                jax.ShapeDtypeStruct((b, n, LANES, s), jnp.bfloat16))

    outs = (heads(MLA_HEADS), heads(MLA_HEADS), heads_t(MLA_HEADS),
            heads(SWA_HEADS), heads(SWA_KV_HEADS), heads_t(SWA_KV_HEADS))
    consts = (pre_g, w_proj, w_vst, qn_g, wq, kvn_g, wk, wvt)
    return pl.pallas_call(
        _proj_kernel,
        grid=(b, s // tm),
        in_specs=[row] + [tab_spec] * len(tabs) + [_const_spec(c.shape) for c in consts],
        out_specs=[o[0] for o in outs],
        out_shape=[o[1] for o in outs],
        compiler_params=_params(2),
        name="proj",
    )(x1, *tabs, *consts)


LOG2_E = 1.4426950408889634


def _mla_kernel(q_ref, k_ref, vt_ref, kl_ref, vtl_ref, o_ref, s_a, s_b, lead_a, lead_b, ot_scr,
                *, tq, ck):
    seq = k_ref.shape[2]
    n_chunks = seq // ck
    n_tiles = seq // tq
    sublanes = 8
    lead_ok = lax.broadcasted_iota(jnp.int32, (BLOCK, 1), 0) >= LEAD_PAD

    def fold_max(s):
        return jnp.max(s.reshape(s.shape[0] // sublanes, sublanes, tq), axis=0)

    def rows(t):
        return pl.ds(pl.multiple_of(t * tq, tq), tq)

    def score_chunk(j, q, c, s_new, mx):
        s = _dot_nt(k_ref[0, j, c * ck:(c + 1) * ck, :], q)
        s_new[c] = s
        return jnp.maximum(mx, fold_max(s))

    def value_chunk(j, c, s_old, m, acc):
        p = jnp.exp2(s_old[c] - m).astype(jnp.bfloat16)
        return acc + _dot(vt_ref[0, j, :, c * ck:(c + 1) * ck], p)

    def emit(j, t, acc):
        o_t = acc[:MLA_V] * (1.0 / acc[MLA_V:MLA_V + 1])
        if j % 2 == 0:
            ot_scr[t] = o_t
        else:
            both = jnp.concatenate([ot_scr[t], o_t], axis=0)
            slab = slice((j // 2) * LANES, (j // 2 + 1) * LANES)
            o_ref[0, rows(t), slab] = both.T.astype(o_ref.dtype)

    def lead_values(j, lead_old, mx_old):
        m = jnp.max(mx_old, axis=0, keepdims=True)
        return m, _dot(vtl_ref[0, j], jnp.exp2(lead_old[...] - m).astype(jnp.bfloat16))

    def stage(new, old, older, buf_new, buf_old, mx_old, acc_older):
        if older is not None:
            emit(*older, acc_older)
        j, t = new
        s_new, lead_new = buf_new
        q = q_ref[0, j, rows(t), :]
        s_lead = jnp.where(lead_ok, _dot_nt(kl_ref[0, j], q), NEG)
        lead_new[...] = s_lead
        mx, acc = fold_max(s_lead), None
        if old is not None:
            m, acc = lead_values(old[0], buf_old[1], mx_old)
        for c in range(n_chunks):
            mx = score_chunk(j, q, c, s_new, mx)
            if old is not None:
                acc = value_chunk(old[0], c, buf_old[0], m, acc)
        return mx, acc

    buf_a, buf_b = (s_a, lead_a), (s_b, lead_b)
    n_heads = q_ref.shape[1]
    end = n_tiles - 1
    for j in range(n_heads):
        if j == 0:
            mx, _ = stage((0, 0), None, None, buf_a, None, None, None)
            mx, acc = stage((0, 1), (0, 0), None, buf_b, buf_a, mx, None)
        else:
            mx, acc = stage((j, 0), (j - 1, end), (j - 1, end - 1), buf_a, buf_b, mx, acc)
            mx, acc = stage((j, 1), (j, 0), (j - 1, end), buf_b, buf_a, mx, acc)

        def pair(i, carry, j=j):
            t = 2 * i + 2
            mx, acc = stage((j, t), (j, t - 1), (j, t - 2), buf_a, buf_b, *carry)
            return stage((j, t + 1), (j, t), (j, t - 1), buf_b, buf_a, mx, acc)

        mx, acc = lax.fori_loop(0, n_tiles // 2 - 1, pair, (mx, acc))
    last = n_heads - 1
    emit(last, end - 1, acc)
    m, acc = lead_values(last, lead_b, mx)
    for c in range(n_chunks):
        acc = value_chunk(last, c, s_b, m, acc)
    emit(last, end, acc)


def _mla_heads_per_step(nh, s, tq, ck):
    bf16_bytes, f32_bytes, pipeline_buffers = 2, 4, 2
    scratch = 2 * (s + BLOCK) * tq * f32_bytes + s * MLA_V * f32_bytes
    in_flight = 2 * ck * tq * f32_bytes
    for hp in range(nh, 0, -2):
        per_head = (3 * s * LANES + s * MLA_V) * bf16_bytes
        if nh % hp == 0 and pipeline_buffers * hp * per_head + scratch + in_flight <= VMEM_LIMIT:
            return hp
    raise ValueError("sequence too long for the resident-key MLA kernel")


def _mla(q, k, vt, k_lead, vt_lead, tq, ck):
    b, nh, s, _ = q.shape
    hp = _mla_heads_per_step(nh, s, tq, ck)
    assert s % ck == 0 and s % (2 * tq) == 0
    seq_spec = pl.BlockSpec((1, hp, s, LANES), lambda i, p: (i, p, 0, 0))
    vt_spec = pl.BlockSpec((1, hp, LANES, s), lambda i, p: (i, p, 0, 0))
    lead_spec = pl.BlockSpec((1, hp, BLOCK, LANES), lambda i, p: (0, p, 0, 0))
    score_buf = pltpu.VMEM((s // ck, ck, tq), jnp.float32)
    lead_buf = pltpu.VMEM((BLOCK, tq), jnp.float32)
    return pl.pallas_call(
        functools.partial(_mla_kernel, tq=tq, ck=ck),
        grid=(b, nh // hp),
        in_specs=[seq_spec, seq_spec, vt_spec, lead_spec, lead_spec],
        out_specs=pl.BlockSpec((1, s, hp * MLA_V), lambda i, p: (i, 0, p)),
        out_shape=jax.ShapeDtypeStruct((b, s, nh * MLA_V), jnp.bfloat16),
        scratch_shapes=[score_buf, score_buf, lead_buf, lead_buf,
                        pltpu.VMEM((s // tq, MLA_V, tq), jnp.float32)],
        compiler_params=_params(2),
        name="mla",
    )(q, k, vt, k_lead, vt_lead)


def _window_kernel(sink_ref, q_ref, kl_ref, kc_ref, kr_ref, km_ref,
                   vl_ref, vc_ref, vr_ref, vm_ref, o_ref):
    tq = q_ref.shape[2]
    n_blk = tq // BLOCK
    t = pl.program_id(1)
    n_t = pl.num_programs(1)
    n_keys = 4 * BLOCK
    n_cols = SWA_GROUP * BLOCK
    key = lax.broadcasted_iota(jnp.int32, (n_keys, n_cols), 0)
    query = lax.broadcasted_iota(jnp.int32, (n_keys, n_cols), 1) % BLOCK
    in_left = jnp.logical_and(key < BLOCK, key >= query)
    in_right = jnp.logical_and(jnp.logical_and(key >= 2 * BLOCK, key < 3 * BLOCK),
                               key - 2 * BLOCK <= query)
    in_rest = jnp.logical_or(jnp.logical_and(key >= BLOCK, key < 2 * BLOCK),
                             key >= 3 * BLOCK + LEAD_PAD)
    sublanes = 8

    def keys_of(blk, left_ref, center_ref, right_ref, lead_ref, g):
        first, last = blk == 0, blk == n_blk - 1
        return jnp.concatenate(
            [left_ref[0, g] if first else center_ref[0, g, (blk - 1) * BLOCK:blk * BLOCK, :],
             center_ref[0, g, blk * BLOCK:(blk + 1) * BLOCK, :],
             right_ref[0, g] if last else center_ref[0, g, (blk + 1) * BLOCK:(blk + 2) * BLOCK, :],
             lead_ref[0, g]], axis=0)

    def values_t_of(blk, g):
        first, last = blk == 0, blk == n_blk - 1
        return jnp.concatenate(
            [vl_ref[0, g] if first else vc_ref[0, g, :, (blk - 1) * BLOCK:blk * BLOCK],
             vc_ref[0, g, :, blk * BLOCK:(blk + 1) * BLOCK],
             vr_ref[0, g] if last else vc_ref[0, g, :, (blk + 1) * BLOCK:(blk + 2) * BLOCK],
             vm_ref[0, g]], axis=1)

    def scores(g, blk):
        ok_left = jnp.logical_and(in_left, t > 0) if blk == 0 else in_left
        ok_right = jnp.logical_and(in_right, t < n_t - 1) if blk == n_blk - 1 else in_right
        ok = jnp.logical_or(jnp.logical_or(ok_left, ok_right), in_rest)
        q = q_ref[0, g * SWA_GROUP:(g + 1) * SWA_GROUP, blk * BLOCK:(blk + 1) * BLOCK, :]
        s = _dot_nt(keys_of(blk, kl_ref, kc_ref, kr_ref, km_ref, g),
                    q.reshape(n_cols, LANES))
        return jnp.where(ok, s, NEG)

    def finish(g, blk, s):
        sink = jnp.concatenate(
            [jnp.full((1, BLOCK), sink_ref[g * SWA_GROUP + r] * LOG2_E, jnp.float32)
             for r in range(SWA_GROUP)], axis=1)
        folded = jnp.max(s.reshape(n_keys // sublanes, sublanes, n_cols), axis=0)
        m = jnp.maximum(jnp.max(folded, axis=0, keepdims=True), sink)
        p = jnp.exp2(s - m).astype(jnp.bfloat16)
        acc = _dot(values_t_of(blk, g), p)
        o_t = acc[:SWA_HEAD_DIM] * (
            1.0 / (acc[SWA_HEAD_DIM:SWA_HEAD_DIM + 1] + jnp.exp2(sink - m)))
        for pair in range(SWA_GROUP // 2):
            both = jnp.concatenate(
                [o_t[:, (2 * pair) * BLOCK:(2 * pair + 1) * BLOCK],
                 o_t[:, (2 * pair + 1) * BLOCK:(2 * pair + 2) * BLOCK]], axis=0)
            slab = g * (SWA_GROUP // 2) + pair
            o_ref[0, blk * BLOCK:(blk + 1) * BLOCK, slab * LANES:(slab + 1) * LANES] = (
                both.T.astype(o_ref.dtype))

    units = [(g, blk) for g in range(SWA_KV_HEADS) for blk in range(n_blk)]
    ahead = 3
    pending = [scores(*u) for u in units[:ahead]]
    for i, unit in enumerate(units):
        if i + ahead < len(units):
            pending.append(scores(*units[i + ahead]))
        finish(*unit, pending.pop(0))


def _window(sink, q, k, v, k_lead, v_lead, tq):
    b, nh, s, _ = q.shape
    n_blk = s // BLOCK
    per = tq // BLOCK

    def specs(n, transposed):
        def spec(rows, index):
            if transposed:
                return pl.BlockSpec((1, n, LANES, rows), lambda i, t: (*index(i, t)[:2], 0,
                                                                       index(i, t)[2]))
            return pl.BlockSpec((1, n, rows, LANES), lambda i, t: (*index(i, t), 0))

        return [spec(BLOCK, lambda i, t: (i, 0, jnp.maximum(t * per - 1, 0))),
                spec(tq, lambda i, t: (i, 0, t)),
                spec(BLOCK, lambda i, t: (i, 0, jnp.minimum((t + 1) * per, n_blk - 1))),
                spec(BLOCK, lambda i, t: (0, 0, 0))]

    smem = pl.BlockSpec(memory_space=pltpu.SMEM)
    return pl.pallas_call(
        _window_kernel,
        grid=(b, s // tq),
        in_specs=([smem, specs(nh, False)[1]] + specs(SWA_KV_HEADS, False)
                  + specs(SWA_KV_HEADS, True)),
        out_specs=pl.BlockSpec((1, tq, nh * SWA_HEAD_DIM), lambda i, t: (i, t, 0)),
        out_shape=jax.ShapeDtypeStruct((b, s, nh * SWA_HEAD_DIM), jnp.bfloat16),
        compiler_params=_params(2),
        name="window",
    )(sink, q, k, k, k, k_lead, v, v, v, v_lead)


def _mixout_kernel(x_ref, oa_ref, ob_ref, pre_g_ref, wg_ref, woa_ref, wob_ref, wout_ref,
                   post_g_ref, o_ref):
    half = x_ref.shape[1] // 2
    for rows in (slice(0, half), slice(half, 2 * half)):
        x1 = x_ref[0, rows, :]
        h = _rms(x1, pre_g_ref[...]).astype(jnp.bfloat16)
        y_a = _dot(oa_ref[0, rows, :], woa_ref[...])
        y_b = _dot(ob_ref[0, rows, :], wob_ref[...])
        merged = (jax.nn.sigmoid(_dot(h, wg_ref[:, :D_MODEL])) * y_a
                  + jax.nn.sigmoid(_dot(h, wg_ref[:, D_MODEL:])) * y_b)
        o_ref[0, rows, :] = x1 + _rms(_dot(merged.astype(jnp.bfloat16), wout_ref[...]),
                                      post_g_ref[...])


def _mixout(x1, o_a, o_b, pre_g, wg, woa, wob, wout, post_g, tm):
    b, s, d = x1.shape
    row = pl.BlockSpec((1, tm, d), lambda i, t: (i, t, 0))
    half = pl.BlockSpec((1, tm, o_a.shape[2]), lambda i, t: (i, t, 0))
    return pl.pallas_call(
        _mixout_kernel,
        grid=(b, s // tm),
        in_specs=[row, half, half, _const_spec(pre_g.shape), _const_spec(wg.shape),
                  _const_spec(woa.shape), _const_spec(wob.shape), _const_spec(wout.shape),
                  _const_spec(post_g.shape)],
        out_specs=row,
        out_shape=jax.ShapeDtypeStruct(x1.shape, jnp.float32),
        compiler_params=_params(2),
        name="mixout",
    )(x1, o_a, o_b, pre_g, wg, woa, wob, wout, post_g)


def _pack_weights(w_in, w_uq, w_ukv):
    bf16 = jnp.bfloat16
    d = w_in.shape[0]
    o = 0
    parts = []
    for n in (MLA_Q_LORA, MLA_KV_LORA, MLA_ROPE, SWA_HEADS * SWA_HEAD_DIM,
              SWA_KV_HEADS * SWA_HEAD_DIM, SWA_KV_HEADS * SWA_HEAD_DIM, D_MODEL, D_MODEL):
        parts.append(w_in[:, o:o + n])
        o += n
    w_cq, w_ckv, w_kr, w_qs, w_ks, w_vs, w_ga, w_gb = parts

    def mla_slab(nope, rope):
        half = MLA_ROPE // 2

        def interleave(x, y):
            return jnp.stack([x, y], axis=-1).reshape(x.shape[:-1] + (2 * half,))

        tail = jnp.zeros(nope.shape[:-1] + (LANES - MLA_NOPE - MLA_ROPE,), nope.dtype)
        return jnp.concatenate(
            [interleave(rope[..., :half], nope[..., :half]), nope[..., half:MLA_NOPE - half],
             interleave(rope[..., half:], nope[..., MLA_NOPE - half:]), tail], axis=-1)

    def window_pairs(w):
        quarter = SWA_HEAD_DIM // 2
        w = w.reshape(d, -1, 2, 2, quarter)
        return jnp.swapaxes(w, 2, 3).reshape(d, -1)

    kr_slab = mla_slab(jnp.zeros((d, MLA_NOPE), w_kr.dtype), w_kr)
    w_proj = jnp.concatenate([w_cq, w_ckv, window_pairs(w_qs), window_pairs(w_ks), kr_slab],
                             axis=1).astype(bf16)
    w_gate = jnp.concatenate([w_ga, w_gb], axis=1).astype(bf16)
    w_vst = w_vs.T.astype(bf16)

    r = w_uq.shape[0]
    uq = w_uq.reshape(r, MLA_HEADS, MLA_NOPE + MLA_ROPE)
    wq = mla_slab(uq[..., :MLA_NOPE], uq[..., MLA_NOPE:]).reshape(r, -1).astype(bf16)

    r = w_ukv.shape[0]
    ukv = w_ukv.reshape(r, MLA_HEADS, MLA_NOPE + MLA_V)
    no_rope = jnp.zeros((r, MLA_HEADS, MLA_ROPE), w_ukv.dtype)
    wk = mla_slab(ukv[..., :MLA_NOPE], no_rope).reshape(r, -1).astype(bf16)
    zeros = jnp.zeros((r, MLA_HEADS, LANES - MLA_V), w_ukv.dtype)
    v_slab = jnp.concatenate([ukv[..., MLA_NOPE:], zeros], axis=-1)
    wvt = v_slab.reshape(r, -1).T.astype(bf16)
    return w_proj, w_gate, wq, wk, wvt, w_vst


def _rope_tables(pos):
    pos = pos.astype(jnp.float32)[:, None]
    lane = jnp.arange(LANES)[None, :]
    half = SWA_HEAD_DIM // 2
    inv = ROPE_THETA ** (-jnp.arange(half, dtype=jnp.float32) / half)
    ang = pos * inv[lane % half]
    return jnp.cos(ang), jnp.where(lane < ROPE_PARTNER, -jnp.sin(ang), jnp.sin(ang))


def _row_tile(s):
    return min(s, 512)


def kernel(x_prompt, x_sample, meta_tokens, ffn1_pre_g, ffn1_w_in, ffn1_w_out, ffn1_post_g,
           mix_pre_g, w_in, q_norm_g, w_uq, kv_norm_g, w_ukv, sink, w_o_a, w_o_b, w_out,
           mix_post_g, ffn2_pre_g, ffn2_w_in, ffn2_w_out, ffn2_post_g):
    assert ffn1_w_in.shape[0] == 1, "single layer"
    bf16 = jnp.bfloat16
    f1 = (ffn1_pre_g, ffn1_w_in[0].astype(bf16), ffn1_w_out[0].astype(bf16), ffn1_post_g)
    f2 = (ffn2_pre_g, ffn2_w_in[0].astype(bf16), ffn2_w_out[0].astype(bf16), ffn2_post_g)
    w_proj, w_gate, wq, wk, wvt, w_vst = _pack_weights(w_in[0], w_uq[0], w_ukv[0])
    woa, wob, wout = w_o_a[0].astype(bf16), w_o_b[0].astype(bf16), w_out[0].astype(bf16)
    sink = sink[0]

    def front(x, tabs):
        tm = _row_tile(x.shape[1])
        x1 = _ffn(x, *f1, tm)
        return x1, _proj(x1, tabs, mix_pre_g, w_proj, w_vst, q_norm_g, wq, kv_norm_g, wk, wvt,
                         tm)

    lead = jnp.concatenate([jnp.zeros((LEAD_PAD, D_MODEL), x_prompt.dtype),
                            meta_tokens.astype(x_prompt.dtype)], axis=0)[None]
    _, (_, ka_lead, va_lead, _, kw_lead, vw_lead) = front(
        lead, _rope_tables(jnp.arange(BLOCK) - LEAD_PAD))
    seq_tabs = _rope_tables(jnp.arange(max(x_prompt.shape[1], x_sample.shape[1])) + N_META)

    def trunk(x):
        s = x.shape[1]
        tm = _row_tile(s)
        x1, (qa, ka, va, qw, kw, vw) = front(x, seq_tabs)
        o_a = _mla(qa, ka, va, ka_lead, va_lead, tq=MXU_COLS, ck=1024)
        o_b = _window(sink, qw, kw, vw, kw_lead, vw_lead, tq=8 * BLOCK)
        x2 = _mixout(x1, o_a, o_b, mix_pre_g, w_gate, woa, wob, wout, mix_post_g, tm)
        return _ffn(x2, *f2, tm)

    return (trunk(x_prompt), trunk(x_sample))
```

```python
import functools

import jax
import jax.numpy as jnp
from jax import lax
from jax.experimental import pallas as pl
from jax.experimental.pallas import tpu as pltpu

D_MODEL = 1024
N_META = 16
BLOCK = 128
WINDOW = 128
assert WINDOW == BLOCK, "the band masks below are written for a window of one block"
LEAD_PAD = BLOCK - N_META
ROPE_THETA = 10000.0
EPS = 1e-6
NEG = -1e30
MLA_HEADS = 8
MLA_Q_LORA = 384
MLA_KV_LORA = 256
MLA_NOPE = 64
MLA_ROPE = 32
MLA_V = 64
SWA_HEADS = 8
SWA_KV_HEADS = 2
SWA_HEAD_DIM = 64
SWA_GROUP = SWA_HEADS // SWA_KV_HEADS
D_FF = 2816

LANES = 128
ROPE_PARTNER = LANES // 2
assert SWA_HEAD_DIM == 2 * MLA_ROPE, "MLA rope frequencies must be every second window one"
MLA_SCALE = (MLA_NOPE + MLA_ROPE) ** -0.5
SWA_SCALE = SWA_HEAD_DIM ** -0.5
V7X_VMEM_BYTES = 64 * 1024 * 1024
VMEM_LIMIT = (V7X_VMEM_BYTES * 7) // 8

_C_Q = (0, MLA_Q_LORA)
_C_KV = (_C_Q[1], _C_Q[1] + MLA_KV_LORA)
_C_QS = (_C_KV[1], _C_KV[1] + SWA_HEADS * SWA_HEAD_DIM)
_C_KS = (_C_QS[1], _C_QS[1] + LANES)
_C_KR = (_C_KS[1], _C_KS[1] + LANES)

MXU_COLS = 256
FFN_CHUNKS = ((0, 6 * MXU_COLS), (6 * MXU_COLS, D_FF))
assert all((hi - lo) % MXU_COLS == 0 for lo, hi in FFN_CHUNKS)


def _const_spec(shape):
    zeros = (0,) * len(shape)
    return pl.BlockSpec(shape, lambda *_: zeros, pipeline_mode=pl.Buffered(1))


def _params(n_axes):
    return pltpu.CompilerParams(
        dimension_semantics=("parallel",) * n_axes, vmem_limit_bytes=VMEM_LIMIT)


def _rms(x, g):
    y = x * lax.rsqrt(jnp.mean(x * x, axis=-1, keepdims=True) + EPS)
    return y * g


def _dot(a, b):
    return jnp.dot(a, b, preferred_element_type=jnp.float32)


def _dot_nt(a, b):
    return lax.dot_general(a, b, (((1,), (1,)), ((), ())),
                           preferred_element_type=jnp.float32)


def _swiglu_residual(x, pre_g, w_in_ref, w_out_ref, post_g):
    h = _rms(x, pre_g).astype(jnp.bfloat16)
    acc = None
    for lo, hi in FFN_CHUNKS:
        g = _dot(h, w_in_ref[:, lo:hi])
        u = _dot(h, w_in_ref[:, D_FF + lo:D_FF + hi])
        a = (g * jax.nn.sigmoid(g) * u).astype(jnp.bfloat16)
        part = _dot(a, w_out_ref[lo:hi, :])
        acc = part if acc is None else acc + part
    return x + 0.5 * _rms(acc, post_g)


def _ffn_kernel(x_ref, pre_g_ref, w_in_ref, w_out_ref, post_g_ref, o_ref):
    half = x_ref.shape[1] // 2
    for rows in (slice(0, half), slice(half, 2 * half)):
        o_ref[0, rows, :] = _swiglu_residual(x_ref[0, rows, :], pre_g_ref[...], w_in_ref,
                                             w_out_ref, post_g_ref[...])


def _ffn(x, pre_g, w_in, w_out, post_g, tm):
    b, s, d = x.shape
    row = pl.BlockSpec((1, tm, d), lambda i, t: (i, t, 0))
    return pl.pallas_call(
        _ffn_kernel,
        grid=(b, s // tm),
        in_specs=[row, _const_spec(pre_g.shape), _const_spec(w_in.shape),
                  _const_spec(w_out.shape), _const_spec(post_g.shape)],
        out_specs=row,
        out_shape=jax.ShapeDtypeStruct(x.shape, jnp.float32),
        compiler_params=_params(2),
        name="ffn",
    )(x, pre_g, w_in, w_out, post_g)


def _proj_kernel(x_ref, cw_ref, sw_ref, pre_g_ref, w_ref, wvst_ref, qn_g_ref,
                 wq_ref, kvn_g_ref, wk_ref, wvt_ref,
                 qa_ref, ka_ref, vt_ref, qw_ref, kw_ref, vw_ref):
    bf16 = jnp.bfloat16
    h = _rms(x_ref[0], pre_g_ref[...]).astype(bf16)
    cw_t, sw_t = cw_ref[...], sw_ref[...]
    lane = lax.broadcasted_iota(jnp.int32, (1, LANES), 1)
    mla_rope_lane = jnp.logical_and(lane % ROPE_PARTNER < MLA_ROPE, lane % 2 == 0)
    ck_t = jnp.where(mla_rope_lane, cw_t, 1.0)
    sk_t = jnp.where(mla_rope_lane, sw_t, 0.0)

    proj = _dot(h, w_ref[...])

    def cols(rng):
        return proj[:, rng[0]:rng[1]]

    def rope(x, cos, sin_signed):
        return x * cos + pltpu.roll(x, ROPE_PARTNER, 1) * sin_signed

    qn = _rms(cols(_C_Q), qn_g_ref[...]).astype(bf16)
    qq = _dot(qn, wq_ref[...])
    for hd in range(MLA_HEADS):
        q_rope = rope(qq[:, hd * LANES:(hd + 1) * LANES], ck_t, sk_t)
        qa_ref[0, hd] = (q_rope * (MLA_SCALE * LOG2_E)).astype(bf16)

    kvn = _rms(cols(_C_KV), kvn_g_ref[...]).astype(bf16)
    kk = _dot(kvn, wk_ref[...])
    k_rope = rope(cols(_C_KR), ck_t, sk_t)
    for hd in range(MLA_HEADS):
        ka_ref[0, hd] = (kk[:, hd * LANES:(hd + 1) * LANES] + k_rope).astype(bf16)

    vt = _dot_nt(wvt_ref[...], kvn)
    is_v_row = lax.broadcasted_iota(jnp.int32, (LANES, 1), 0) < MLA_V
    for hd in range(MLA_HEADS):
        vt_ref[0, hd] = jnp.where(is_v_row, vt[hd * LANES:(hd + 1) * LANES], 1.0).astype(bf16)

    quarter = SWA_HEAD_DIM // 2
    even_lanes = lane % SWA_HEAD_DIM < quarter
    qs = cols(_C_QS)
    for pair in range(SWA_HEADS // 2):
        q_pair = rope(qs[:, pair * LANES:(pair + 1) * LANES], cw_t, sw_t) * (SWA_SCALE * LOG2_E)
        qw_ref[0, 2 * pair] = jnp.where(even_lanes, q_pair, 0.0).astype(bf16)
        qw_ref[0, 2 * pair + 1] = jnp.where(even_lanes, 0.0, q_pair).astype(bf16)

    k_pair = rope(cols(_C_KS), cw_t, sw_t)
    from_below = pltpu.roll(k_pair, quarter, 1)
    from_above = pltpu.roll(k_pair, LANES - quarter, 1)
    kw_ref[0, 0] = jnp.where(even_lanes, k_pair, from_below).astype(bf16)
    kw_ref[0, 1] = jnp.where(even_lanes, from_above, k_pair).astype(bf16)

    vtw = _dot_nt(wvst_ref[...], h)
    ones = jnp.ones((SWA_HEAD_DIM, vtw.shape[1]), jnp.float32)
    for g in range(SWA_KV_HEADS):
        v_g = vtw[g * SWA_HEAD_DIM:(g + 1) * SWA_HEAD_DIM]
        vw_ref[0, g] = jnp.concatenate([v_g, ones], axis=0).astype(bf16)


def _proj(x1, tabs, pre_g, w_proj, w_vst, qn_g, wq, kvn_g, wk, wvt, tm):
    b, s, d = x1.shape
    row = pl.BlockSpec((1, tm, d), lambda i, t: (i, t, 0))
    tab_spec = pl.BlockSpec((tm, LANES), lambda i, t: (t, 0))

    def heads(n):
        return (pl.BlockSpec((1, n, tm, LANES), lambda i, t: (i, 0, t, 0)),
                jax.ShapeDtypeStruct((b, n, s, LANES), jnp.bfloat16))

    def heads_t(n):
        return (pl.BlockSpec((1, n, LANES, tm), lambda i, t: (i, 0, 0, t)),
                jax.ShapeDtypeStruct((b, n, LANES, s), jnp.bfloat16))

    outs = (heads(MLA_HEADS), heads(MLA_HEADS), heads_t(MLA_HEADS),
            heads(SWA_HEADS), heads(SWA_KV_HEADS), heads_t(SWA_KV_HEADS))
    consts = (pre_g, w_proj, w_vst, qn_g, wq, kvn_g, wk, wvt)
    return pl.pallas_call(
        _proj_kernel,
        grid=(b, s // tm),
        in_specs=[row] + [tab_spec] * len(tabs) + [_const_spec(c.shape) for c in consts],
        out_specs=[o[0] for o in outs],
        out_shape=[o[1] for o in outs],
        compiler_params=_params(2),
        name="proj",
    )(x1, *tabs, *consts)


N_FFN_CONSTS = 4


def _front_kernel(x_ref, *refs):
    ffn_consts = refs[:N_FFN_CONSTS]
    tabs_and_consts = refs[N_FFN_CONSTS:-7]
    x1_ref, outs = refs[-7], refs[-6:]
    _ffn_kernel(x_ref, *ffn_consts, x1_ref)
    _proj_kernel(x1_ref, *tabs_and_consts, *outs)


def _front(x, ffn_consts, tabs, pre_g, w_proj, w_vst, qn_g, wq, kvn_g, wk, wvt, tm):
    b, s, d = x.shape
    row = pl.BlockSpec((1, tm, d), lambda i, t: (i, t, 0))
    tab_spec = pl.BlockSpec((tm, LANES), lambda i, t: (t, 0))

    def heads(n):
        return (pl.BlockSpec((1, n, tm, LANES), lambda i, t: (i, 0, t, 0)),
                jax.ShapeDtypeStruct((b, n, s, LANES), jnp.bfloat16))

    def heads_t(n):
        return (pl.BlockSpec((1, n, LANES, tm), lambda i, t: (i, 0, 0, t)),
                jax.ShapeDtypeStruct((b, n, LANES, s), jnp.bfloat16))

    outs = ((row, jax.ShapeDtypeStruct(x.shape, jnp.float32)),
            heads(MLA_HEADS), heads(MLA_HEADS), heads_t(MLA_HEADS),
            heads(SWA_HEADS), heads(SWA_KV_HEADS), heads_t(SWA_KV_HEADS))
    consts = (pre_g, w_proj, w_vst, qn_g, wq, kvn_g, wk, wvt)
    assert len(ffn_consts) == N_FFN_CONSTS
    res = pl.pallas_call(
        _front_kernel,
        grid=(b, s // tm),
        in_specs=([row] + [_const_spec(c.shape) for c in ffn_consts] + [tab_spec] * len(tabs)
                  + [_const_spec(c.shape) for c in consts]),
        out_specs=[o[0] for o in outs],
        out_shape=[o[1] for o in outs],
        compiler_params=_params(2),
        name="front",
    )(x, *ffn_consts, *tabs, *consts)
    return res[0], res[1:]


LOG2_E =1.4426950408889634


def _mla_kernel(q_ref, k_ref, vt_ref, kl_ref, vtl_ref, o_ref, s_a, s_b, lead_a, lead_b, ot_scr,
                *, tq, ck):
    seq = k_ref.shape[2]
    n_chunks = seq // ck
    n_tiles = seq // tq
    sublanes = 8
    lead_ok = lax.broadcasted_iota(jnp.int32, (BLOCK, 1), 0) >= LEAD_PAD

    def fold_max(s):
        return jnp.max(s.reshape(s.shape[0] // sublanes, sublanes, tq), axis=0)

    def rows(t):
        return pl.ds(pl.multiple_of(t * tq, tq), tq)

    def score_chunk(j, q, c, s_new, mx):
        s = _dot_nt(k_ref[0, j, c * ck:(c + 1) * ck, :], q)
        s_new[c] = s
        return jnp.maximum(mx, fold_max(s))

    def value_chunk(j, c, s_old, m, acc):
        p = jnp.exp2(s_old[c] - m).astype(jnp.bfloat16)
        return acc + _dot(vt_ref[0, j, :, c * ck:(c + 1) * ck], p)

    def emit(j, t, acc):
        o_t = acc[:MLA_V] * (1.0 / acc[MLA_V:MLA_V + 1])
        if j % 2 == 0:
            ot_scr[t] = o_t
        else:
            both = jnp.concatenate([ot_scr[t], o_t], axis=0)
            slab = slice((j // 2) * LANES, (j // 2 + 1) * LANES)
            o_ref[0, rows(t), slab] = both.T.astype(o_ref.dtype)

    def lead_values(j, lead_old, mx_old):
        m = jnp.max(mx_old, axis=0, keepdims=True)
        return m, _dot(vtl_ref[0, j], jnp.exp2(lead_old[...] - m).astype(jnp.bfloat16))

    def stage(new, old, older, buf_new, buf_old, mx_old, acc_older):
        if older is not None:
            emit(*older, acc_older)
        j, t = new
        s_new, lead_new = buf_new
        q = q_ref[0, j, rows(t), :]
        s_lead = jnp.where(lead_ok, _dot_nt(kl_ref[0, j], q), NEG)
        lead_new[...] = s_lead
        mx, acc = fold_max(s_lead), None
        if old is not None:
            m, acc = lead_values(old[0], buf_old[1], mx_old)
        for c in range(n_chunks):
            mx = score_chunk(j, q, c, s_new, mx)
            if old is not None:
                acc = value_chunk(old[0], c, buf_old[0], m, acc)
        return mx, acc

    buf_a, buf_b = (s_a, lead_a), (s_b, lead_b)
    n_heads = q_ref.shape[1]
    end = n_tiles - 1
    for j in range(n_heads):
        if j == 0:
            mx, _ = stage((0, 0), None, None, buf_a, None, None, None)
            mx, acc = stage((0, 1), (0, 0), None, buf_b, buf_a, mx, None)
        else:
            mx, acc = stage((j, 0), (j - 1, end), (j - 1, end - 1), buf_a, buf_b, mx, acc)
            mx, acc = stage((j, 1), (j, 0), (j - 1, end), buf_b, buf_a, mx, acc)

        def pair(i, carry, j=j):
            t = 2 * i + 2
            mx, acc = stage((j, t), (j, t - 1), (j, t - 2), buf_a, buf_b, *carry)
            return stage((j, t + 1), (j, t), (j, t - 1), buf_b, buf_a, mx, acc)

        mx, acc = lax.fori_loop(0, n_tiles // 2 - 1, pair, (mx, acc))
    last = n_heads - 1
    emit(last, end - 1, acc)
    m, acc = lead_values(last, lead_b, mx)
    for c in range(n_chunks):
        acc = value_chunk(last, c, s_b, m, acc)
    emit(last, end, acc)


def _mla_heads_per_step(nh, s, tq, ck):
    bf16_bytes, f32_bytes, pipeline_buffers = 2, 4, 2
    scratch = 2 * (s + BLOCK) * tq * f32_bytes + s * MLA_V * f32_bytes
    in_flight = 2 * ck * tq * f32_bytes
    for hp in range(nh, 0, -2):
        per_head = (3 * s * LANES + s * MLA_V) * bf16_bytes
        if nh % hp == 0 and pipeline_buffers * hp * per_head + scratch + in_flight <= VMEM_LIMIT:
            return hp
    raise ValueError("sequence too long for the resident-key MLA kernel")


def _mla(q, k, vt, k_lead, vt_lead, tq, ck):
    b, nh, s, _ = q.shape
    hp = _mla_heads_per_step(nh, s, tq, ck)
    assert s % ck == 0 and s % (2 * tq) == 0
    seq_spec = pl.BlockSpec((1, hp, s, LANES), lambda i, p: (i, p, 0, 0))
    vt_spec = pl.BlockSpec((1, hp, LANES, s), lambda i, p: (i, p, 0, 0))
    lead_spec = pl.BlockSpec((1, hp, BLOCK, LANES), lambda i, p: (0, p, 0, 0))
    score_buf = pltpu.VMEM((s // ck, ck, tq), jnp.float32)
    lead_buf = pltpu.VMEM((BLOCK, tq), jnp.float32)
    return pl.pallas_call(
        functools.partial(_mla_kernel, tq=tq, ck=ck),
        grid=(b, nh // hp),
        in_specs=[seq_spec, seq_spec, vt_spec, lead_spec, lead_spec],
        out_specs=pl.BlockSpec((1, s, hp * MLA_V), lambda i, p: (i, 0, p)),
        out_shape=jax.ShapeDtypeStruct((b, s, nh * MLA_V), jnp.bfloat16),
        scratch_shapes=[score_buf, score_buf, lead_buf, lead_buf,
                        pltpu.VMEM((s // tq, MLA_V, tq), jnp.float32)],
        compiler_params=_params(2),
        name="mla",
    )(q, k, vt, k_lead, vt_lead)


def _window_kernel(sink_ref, q_ref, kl_ref, kc_ref, kr_ref, km_ref,
                   vl_ref, vc_ref, vr_ref, vm_ref, o_ref):
    tq = q_ref.shape[2]
    n_blk = tq // BLOCK
    t = pl.program_id(1)
    n_t = pl.num_programs(1)
    n_keys = 4 * BLOCK
    n_cols = SWA_GROUP * BLOCK
    key = lax.broadcasted_iota(jnp.int32, (n_keys, n_cols), 0)
    query = lax.broadcasted_iota(jnp.int32, (n_keys, n_cols), 1) % BLOCK
    in_left = jnp.logical_and(key < BLOCK, key >= query)
    in_right = jnp.logical_and(jnp.logical_and(key >= 2 * BLOCK, key < 3 * BLOCK),
                               key - 2 * BLOCK <= query)
    in_rest = jnp.logical_or(jnp.logical_and(key >= BLOCK, key < 2 * BLOCK),
                             key >= 3 * BLOCK + LEAD_PAD)
    sublanes = 8

    def keys_of(blk, left_ref, center_ref, right_ref, lead_ref, g):
        first, last = blk == 0, blk == n_blk - 1
        return jnp.concatenate(
            [left_ref[0, g] if first else center_ref[0, g, (blk - 1) * BLOCK:blk * BLOCK, :],
             center_ref[0, g, blk * BLOCK:(blk + 1) * BLOCK, :],
             right_ref[0, g] if last else center_ref[0, g, (blk + 1) * BLOCK:(blk + 2) * BLOCK, :],
             lead_ref[0, g]], axis=0)

    def values_t_of(blk, g):
        first, last = blk == 0, blk == n_blk - 1
        return jnp.concatenate(
            [vl_ref[0, g] if first else vc_ref[0, g, :, (blk - 1) * BLOCK:blk * BLOCK],
             vc_ref[0, g, :, blk * BLOCK:(blk + 1) * BLOCK],
             vr_ref[0, g] if last else vc_ref[0, g, :, (blk + 1) * BLOCK:(blk + 2) * BLOCK],
             vm_ref[0, g]], axis=1)

    def scores(g, blk):
        ok_left = jnp.logical_and(in_left, t > 0) if blk == 0 else in_left
        ok_right = jnp.logical_and(in_right, t < n_t - 1) if blk == n_blk - 1 else in_right
        ok = jnp.logical_or(jnp.logical_or(ok_left, ok_right), in_rest)
        q = q_ref[0, g * SWA_GROUP:(g + 1) * SWA_GROUP, blk * BLOCK:(blk + 1) * BLOCK, :]
        s = _dot_nt(keys_of(blk, kl_ref, kc_ref, kr_ref, km_ref, g),
                    q.reshape(n_cols, LANES))
        return jnp.where(ok, s, NEG)

    def finish(g, blk, s):
        sink = jnp.concatenate(
            [jnp.full((1, BLOCK), sink_ref[g * SWA_GROUP + r] * LOG2_E, jnp.float32)
             for r in range(SWA_GROUP)], axis=1)
        folded = jnp.max(s.reshape(n_keys // sublanes, sublanes, n_cols), axis=0)
        m = jnp.maximum(jnp.max(folded, axis=0, keepdims=True), sink)
        p = jnp.exp2(s - m).astype(jnp.bfloat16)
        acc = _dot(values_t_of(blk, g), p)
        o_t = acc[:SWA_HEAD_DIM] * (
            1.0 / (acc[SWA_HEAD_DIM:SWA_HEAD_DIM + 1] + jnp.exp2(sink - m)))
        for pair in range(SWA_GROUP // 2):
            both = jnp.concatenate(
                [o_t[:, (2 * pair) * BLOCK:(2 * pair + 1) * BLOCK],
                 o_t[:, (2 * pair + 1) * BLOCK:(2 * pair + 2) * BLOCK]], axis=0)
            slab = g * (SWA_GROUP // 2) + pair
            o_ref[0, blk * BLOCK:(blk + 1) * BLOCK, slab * LANES:(slab + 1) * LANES] = (
                both.T.astype(o_ref.dtype))

    units = [(g, blk) for g in range(SWA_KV_HEADS) for blk in range(n_blk)]
    ahead = 3
    pending = [scores(*u) for u in units[:ahead]]
    for i, unit in enumerate(units):
        if i + ahead < len(units):
            pending.append(scores(*units[i + ahead]))
        finish(*unit, pending.pop(0))


def _window(sink, q, k, v, k_lead, v_lead, tq):
    b, nh, s, _ = q.shape
    n_blk = s // BLOCK
    per = tq // BLOCK

    def specs(n, transposed):
        def spec(rows, index):
            if transposed:
                return pl.BlockSpec((1, n, LANES, rows), lambda i, t: (*index(i, t)[:2], 0,
                                                                       index(i, t)[2]))
            return pl.BlockSpec((1, n, rows, LANES), lambda i, t: (*index(i, t), 0))

        return [spec(BLOCK, lambda i, t: (i, 0, jnp.maximum(t * per - 1, 0))),
                spec(tq, lambda i, t: (i, 0, t)),
                spec(BLOCK, lambda i, t: (i, 0, jnp.minimum((t + 1) * per, n_blk - 1))),
                spec(BLOCK, lambda i, t: (0, 0, 0))]

    smem = pl.BlockSpec(memory_space=pltpu.SMEM)
    return pl.pallas_call(
        _window_kernel,
        grid=(b, s // tq),
        in_specs=([smem, specs(nh, False)[1]] + specs(SWA_KV_HEADS, False)
                  + specs(SWA_KV_HEADS, True)),
        out_specs=pl.BlockSpec((1, tq, nh * SWA_HEAD_DIM), lambda i, t: (i, t, 0)),
        out_shape=jax.ShapeDtypeStruct((b, s, nh * SWA_HEAD_DIM), jnp.bfloat16),
        compiler_params=_params(2),
        name="window",
    )(sink, q, k, k, k, k_lead, v, v, v, v_lead)


def _mixout_kernel(x_ref, oa_ref, ob_ref, pre_g_ref, wg_ref, woa_ref, wob_ref, wout_ref,
                   post_g_ref, o_ref):
    half = x_ref.shape[1] // 2
    for rows in (slice(0, half), slice(half, 2 * half)):
        x1 = x_ref[0, rows, :]
        h = _rms(x1, pre_g_ref[...]).astype(jnp.bfloat16)
        y_a = _dot(oa_ref[0, rows, :], woa_ref[...])
        y_b = _dot(ob_ref[0, rows, :], wob_ref[...])
        merged = (jax.nn.sigmoid(_dot(h, wg_ref[:, :D_MODEL])) * y_a
                  + jax.nn.sigmoid(_dot(h, wg_ref[:, D_MODEL:])) * y_b)
        o_ref[0, rows, :] = x1 + _rms(_dot(merged.astype(jnp.bfloat16), wout_ref[...]),
                                      post_g_ref[...])


def _mixout(x1, o_a, o_b, pre_g, wg, woa, wob, wout, post_g, tm):
    b, s, d = x1.shape
    row = pl.BlockSpec((1, tm, d), lambda i, t: (i, t, 0))
    half = pl.BlockSpec((1, tm, o_a.shape[2]), lambda i, t: (i, t, 0))
    return pl.pallas_call(
        _mixout_kernel,
        grid=(b, s // tm),
        in_specs=[row, half, half, _const_spec(pre_g.shape), _const_spec(wg.shape),
                  _const_spec(woa.shape), _const_spec(wob.shape), _const_spec(wout.shape),
                  _const_spec(post_g.shape)],
        out_specs=row,
        out_shape=jax.ShapeDtypeStruct(x1.shape, jnp.float32),
        compiler_params=_params(2),
        name="mixout",
    )(x1, o_a, o_b, pre_g, wg, woa, wob, wout, post_g)


def _pack_weights(w_in, w_uq, w_ukv):
    bf16 = jnp.bfloat16
    d = w_in.shape[0]
    o = 0
    parts = []
    for n in (MLA_Q_LORA, MLA_KV_LORA, MLA_ROPE, SWA_HEADS * SWA_HEAD_DIM,
              SWA_KV_HEADS * SWA_HEAD_DIM, SWA_KV_HEADS * SWA_HEAD_DIM, D_MODEL, D_MODEL):
        parts.append(w_in[:, o:o + n])
        o += n
    w_cq, w_ckv, w_kr, w_qs, w_ks, w_vs, w_ga, w_gb = parts

    def mla_slab(nope, rope):
        half = MLA_ROPE // 2

        def interleave(x, y):
            return jnp.stack([x, y], axis=-1).reshape(x.shape[:-1] + (2 * half,))

        tail = jnp.zeros(nope.shape[:-1] + (LANES - MLA_NOPE - MLA_ROPE,), nope.dtype)
        return jnp.concatenate(
            [interleave(rope[..., :half], nope[..., :half]), nope[..., half:MLA_NOPE - half],
             interleave(rope[..., half:], nope[..., MLA_NOPE - half:]), tail], axis=-1)

    def window_pairs(w):
        quarter = SWA_HEAD_DIM // 2
        w = w.reshape(d, -1, 2, 2, quarter)
        return jnp.swapaxes(w, 2, 3).reshape(d, -1)

    kr_slab = mla_slab(jnp.zeros((d, MLA_NOPE), w_kr.dtype), w_kr)
    w_proj = jnp.concatenate([w_cq, w_ckv, window_pairs(w_qs), window_pairs(w_ks), kr_slab],
                             axis=1).astype(bf16)
    w_gate = jnp.concatenate([w_ga, w_gb], axis=1).astype(bf16)
    w_vst = w_vs.T.astype(bf16)

    r = w_uq.shape[0]
    uq = w_uq.reshape(r, MLA_HEADS, MLA_NOPE + MLA_ROPE)
    wq = mla_slab(uq[..., :MLA_NOPE], uq[..., MLA_NOPE:]).reshape(r, -1).astype(bf16)

    r = w_ukv.shape[0]
    ukv = w_ukv.reshape(r, MLA_HEADS, MLA_NOPE + MLA_V)
    no_rope = jnp.zeros((r, MLA_HEADS, MLA_ROPE), w_ukv.dtype)
    wk = mla_slab(ukv[..., :MLA_NOPE], no_rope).reshape(r, -1).astype(bf16)
    zeros = jnp.zeros((r, MLA_HEADS, LANES - MLA_V), w_ukv.dtype)
    v_slab = jnp.concatenate([ukv[..., MLA_NOPE:], zeros], axis=-1)
    wvt = v_slab.reshape(r, -1).T.astype(bf16)
    return w_proj, w_gate, wq, wk, wvt, w_vst


def _rope_tables(pos):
    pos = pos.astype(jnp.float32)[:, None]
    lane = jnp.arange(LANES)[None, :]
    half = SWA_HEAD_DIM // 2
    inv = ROPE_THETA ** (-jnp.arange(half, dtype=jnp.float32) / half)
    ang = pos * inv[lane % half]
    return jnp.cos(ang), jnp.where(lane < ROPE_PARTNER, -jnp.sin(ang), jnp.sin(ang))


def _row_tile(s):
    return min(s, 512)


def kernel(x_prompt, x_sample, meta_tokens, ffn1_pre_g, ffn1_w_in, ffn1_w_out, ffn1_post_g,
           mix_pre_g, w_in, q_norm_g, w_uq, kv_norm_g, w_ukv, sink, w_o_a, w_o_b, w_out,
           mix_post_g, ffn2_pre_g, ffn2_w_in, ffn2_w_out, ffn2_post_g):
    assert ffn1_w_in.shape[0] == 1, "single layer"
    bf16 = jnp.bfloat16
    f1 = (ffn1_pre_g, ffn1_w_in[0].astype(bf16), ffn1_w_out[0].astype(bf16), ffn1_post_g)
    f2 = (ffn2_pre_g, ffn2_w_in[0].astype(bf16), ffn2_w_out[0].astype(bf16), ffn2_post_g)
    w_proj, w_gate, wq, wk, wvt, w_vst = _pack_weights(w_in[0], w_uq[0], w_ukv[0])
    woa, wob, wout = w_o_a[0].astype(bf16), w_o_b[0].astype(bf16), w_out[0].astype(bf16)
    sink = sink[0]

    def front(x, tabs):
        tm = _row_tile(x.shape[1])
        return _front(x, f1, tabs, mix_pre_g, w_proj, w_vst, q_norm_g, wq, kv_norm_g, wk, wvt,
                      tm)

    lead = jnp.concatenate([jnp.zeros((LEAD_PAD, D_MODEL), x_prompt.dtype),
                            meta_tokens.astype(x_prompt.dtype)], axis=0)[None]
    _, (_, ka_lead, va_lead, _, kw_lead, vw_lead) = front(
        lead, _rope_tables(jnp.arange(BLOCK) - LEAD_PAD))
    seq_tabs = _rope_tables(jnp.arange(max(x_prompt.shape[1], x_sample.shape[1])) + N_META)

    def trunk(x):
        s = x.shape[1]
        tm = _row_tile(s)
        x1, (qa, ka, va, qw, kw, vw) = front(x, seq_tabs)
        o_a = _mla(qa, ka, va, ka_lead, va_lead, tq=MXU_COLS, ck=1024)
        o_b = _window(sink, qw, kw, vw, kw_lead, vw_lead, tq=8 * BLOCK)
        x2 = _mixout(x1, o_a, o_b, mix_pre_g, w_gate, woa, wob, wout, mix_post_g, tm)
        return _ffn(x2, *f2, tm)

    return (trunk(x_prompt), trunk(x_sample))
```

```python
import functools

import jax
import jax.numpy as jnp
from jax import lax
from jax.experimental import pallas as pl
from jax.experimental.pallas import tpu as pltpu

D_MODEL = 1024
N_META = 16
BLOCK = 128
WINDOW = 128
assert WINDOW == BLOCK, "the band masks below are written for a window of one block"
LEAD_PAD = BLOCK - N_META
ROPE_THETA = 10000.0
EPS = 1e-6
NEG = -1e30
MLA_HEADS = 8
MLA_Q_LORA = 384
MLA_KV_LORA = 256
MLA_NOPE = 64
MLA_ROPE = 32
MLA_V = 64
SWA_HEADS = 8
SWA_KV_HEADS = 2
SWA_HEAD_DIM = 64
SWA_GROUP = SWA_HEADS // SWA_KV_HEADS
D_FF = 2816

LANES = 128
ROPE_PARTNER = LANES // 2
assert SWA_HEAD_DIM == 2 * MLA_ROPE, "MLA rope frequencies must be every second window one"
MLA_SCALE = (MLA_NOPE + MLA_ROPE) ** -0.5
SWA_SCALE = SWA_HEAD_DIM ** -0.5
V7X_VMEM_BYTES = 64 * 1024 * 1024
VMEM_LIMIT = (V7X_VMEM_BYTES * 7) // 8

_C_Q = (0, MLA_Q_LORA)
_C_KV = (_C_Q[1], _C_Q[1] + MLA_KV_LORA)
_C_QS = (_C_KV[1], _C_KV[1] + SWA_HEADS * SWA_HEAD_DIM)
_C_KS = (_C_QS[1], _C_QS[1] + LANES)
_C_KR = (_C_KS[1], _C_KS[1] + LANES)

MXU_COLS = 256
FFN_CHUNKS = ((0, 6 * MXU_COLS), (6 * MXU_COLS, D_FF))
assert all((hi - lo) % MXU_COLS == 0 for lo, hi in FFN_CHUNKS)


def _const_spec(shape):
    zeros = (0,) * len(shape)
    return pl.BlockSpec(shape, lambda *_: zeros, pipeline_mode=pl.Buffered(1))


def _params(n_axes):
    return pltpu.CompilerParams(
        dimension_semantics=("parallel",) * n_axes, vmem_limit_bytes=VMEM_LIMIT)


def _rms(x, g):
    y = x * lax.rsqrt(jnp.mean(x * x, axis=-1, keepdims=True) + EPS)
    return y * g


def _dot(a, b):
    return jnp.dot(a, b, preferred_element_type=jnp.float32)


def _dot_nt(a, b):
    return lax.dot_general(a, b, (((1,), (1,)), ((), ())),
                           preferred_element_type=jnp.float32)


def _swiglu_residual(x, pre_g, w_in_ref, w_out_ref, post_g):
    h = _rms(x, pre_g).astype(jnp.bfloat16)
    acc = None
    for lo, hi in FFN_CHUNKS:
        g = _dot(h, w_in_ref[:, lo:hi])
        u = _dot(h, w_in_ref[:, D_FF + lo:D_FF + hi])
        a = (g * jax.nn.sigmoid(g) * u).astype(jnp.bfloat16)
        part = _dot(a, w_out_ref[lo:hi, :])
        acc = part if acc is None else acc + part
    return x + 0.5 * _rms(acc, post_g)


def _ffn_kernel(x_ref, pre_g_ref, w_in_ref, w_out_ref, post_g_ref, o_ref):
    half = x_ref.shape[1] // 2
    for rows in (slice(0, half), slice(half, 2 * half)):
        o_ref[0, rows, :] = _swiglu_residual(x_ref[0, rows, :], pre_g_ref[...], w_in_ref,
                                             w_out_ref, post_g_ref[...])


def _ffn(x, pre_g, w_in, w_out, post_g, tm):
    b, s, d = x.shape
    row = pl.BlockSpec((1, tm, d), lambda i, t: (i, t, 0))
    return pl.pallas_call(
        _ffn_kernel,
        grid=(b, s // tm),
        in_specs=[row, _const_spec(pre_g.shape), _const_spec(w_in.shape),
                  _const_spec(w_out.shape), _const_spec(post_g.shape)],
        out_specs=row,
        out_shape=jax.ShapeDtypeStruct(x.shape, jnp.float32),
        compiler_params=_params(2),
        name="ffn",
    )(x, pre_g, w_in, w_out, post_g)


def _proj_kernel(x_ref, cw_ref, sw_ref, pre_g_ref, w_ref, wvst_ref, qn_g_ref,
                 wq_ref, kvn_g_ref, wk_ref, wvt_ref,
                 qa_ref, ka_ref, vt_ref, qw_ref, kw_ref, vw_ref):
    bf16 = jnp.bfloat16
    h = _rms(x_ref[0], pre_g_ref[...]).astype(bf16)
    cw_t, sw_t = cw_ref[...], sw_ref[...]
    lane = lax.broadcasted_iota(jnp.int32, (1, LANES), 1)
    mla_rope_lane = jnp.logical_and(lane % ROPE_PARTNER < MLA_ROPE, lane % 2 == 0)
    ck_t = jnp.where(mla_rope_lane, cw_t, 1.0)
    sk_t = jnp.where(mla_rope_lane, sw_t, 0.0)

    proj = _dot(h, w_ref[...])

    def cols(rng):
        return proj[:, rng[0]:rng[1]]

    def rope(x, cos, sin_signed):
        return x * cos + pltpu.roll(x, ROPE_PARTNER, 1) * sin_signed

    qn = _rms(cols(_C_Q), qn_g_ref[...]).astype(bf16)
    qq = _dot(qn, wq_ref[...])
    for hd in range(MLA_HEADS):
        q_rope = rope(qq[:, hd * LANES:(hd + 1) * LANES], ck_t, sk_t)
        qa_ref[0, hd] = (q_rope * (MLA_SCALE * LOG2_E)).astype(bf16)

    kvn = _rms(cols(_C_KV), kvn_g_ref[...]).astype(bf16)
    kk = _dot(kvn, wk_ref[...])
    k_rope = rope(cols(_C_KR), ck_t, sk_t)
    for hd in range(MLA_HEADS):
        ka_ref[0, hd] = (kk[:, hd * LANES:(hd + 1) * LANES] + k_rope).astype(bf16)

    vt = _dot_nt(wvt_ref[...], kvn)
    is_v_row = lax.broadcasted_iota(jnp.int32, (LANES, 1), 0) < MLA_V
    for hd in range(MLA_HEADS):
        vt_ref[0, hd] = jnp.where(is_v_row, vt[hd * LANES:(hd + 1) * LANES], 1.0).astype(bf16)

    quarter = SWA_HEAD_DIM // 2
    even_lanes = lane % SWA_HEAD_DIM < quarter
    qs = cols(_C_QS)
    for pair in range(SWA_HEADS // 2):
        q_pair = rope(qs[:, pair * LANES:(pair + 1) * LANES], cw_t, sw_t) * (SWA_SCALE * LOG2_E)
        qw_ref[0, 2 * pair] = jnp.where(even_lanes, q_pair, 0.0).astype(bf16)
        qw_ref[0, 2 * pair + 1] = jnp.where(even_lanes, 0.0, q_pair).astype(bf16)

    k_pair = rope(cols(_C_KS), cw_t, sw_t)
    from_below = pltpu.roll(k_pair, quarter, 1)
    from_above = pltpu.roll(k_pair, LANES - quarter, 1)
    kw_ref[0, 0] = jnp.where(even_lanes, k_pair, from_below).astype(bf16)
    kw_ref[0, 1] = jnp.where(even_lanes, from_above, k_pair).astype(bf16)

    vtw = _dot_nt(wvst_ref[...], h)
    ones = jnp.ones((SWA_HEAD_DIM, vtw.shape[1]), jnp.float32)
    for g in range(SWA_KV_HEADS):
        v_g = vtw[g * SWA_HEAD_DIM:(g + 1) * SWA_HEAD_DIM]
        vw_ref[0, g] = jnp.concatenate([v_g, ones], axis=0).astype(bf16)


def _proj(x1, tabs, pre_g, w_proj, w_vst, qn_g, wq, kvn_g, wk, wvt, tm):
    b, s, d = x1.shape
    row = pl.BlockSpec((1, tm, d), lambda i, t: (i, t, 0))
    tab_spec = pl.BlockSpec((tm, LANES), lambda i, t: (t, 0))

    def heads(n):
        return (pl.BlockSpec((1, n, tm, LANES), lambda i, t: (i, 0, t, 0)),
                jax.ShapeDtypeStruct((b, n, s, LANES), jnp.bfloat16))

    def heads_t(n):
        return (pl.BlockSpec((1, n, LANES, tm), lambda i, t: (i, 0, 0, t)),
                jax.ShapeDtypeStruct((b, n, LANES, s), jnp.bfloat16))

    outs = (heads(MLA_HEADS), heads(MLA_HEADS), heads_t(MLA_HEADS),
            heads(SWA_HEADS), heads(SWA_KV_HEADS), heads_t(SWA_KV_HEADS))
    consts = (pre_g, w_proj, w_vst, qn_g, wq, kvn_g, wk, wvt)
    return pl.pallas_call(
        _proj_kernel,
        grid=(b, s // tm),
        in_specs=[row] + [tab_spec] * len(tabs) + [_const_spec(c.shape) for c in consts],
        out_specs=[o[0] for o in outs],
        out_shape=[o[1] for o in outs],
        compiler_params=_params(2),
        name="proj",
    )(x1, *tabs, *consts)


N_FFN_CONSTS = 4


def _front_kernel(x_ref, *refs):
    ffn_consts = refs[:N_FFN_CONSTS]
    tabs_and_consts = refs[N_FFN_CONSTS:-7]
    x1_ref, outs = refs[-7], refs[-6:]
    _ffn_kernel(x_ref, *ffn_consts, x1_ref)
    _proj_kernel(x1_ref, *tabs_and_consts, *outs)


def _front(x, ffn_consts, tabs, pre_g, w_proj, w_vst, qn_g, wq, kvn_g, wk, wvt, tm):
    b, s, d = x.shape
    row = pl.BlockSpec((1, tm, d), lambda i, t: (i, t, 0))
    tab_spec = pl.BlockSpec((tm, LANES), lambda i, t: (t, 0))

    def heads(n):
        return (pl.BlockSpec((1, n, tm, LANES), lambda i, t: (i, 0, t, 0)),
                jax.ShapeDtypeStruct((b, n, s, LANES), jnp.bfloat16))

    def heads_t(n):
        return (pl.BlockSpec((1, n, LANES, tm), lambda i, t: (i, 0, 0, t)),
                jax.ShapeDtypeStruct((b, n, LANES, s), jnp.bfloat16))

    outs = ((row, jax.ShapeDtypeStruct(x.shape, jnp.float32)),
            heads(MLA_HEADS), heads(MLA_HEADS), heads_t(MLA_HEADS),
            heads(SWA_HEADS), heads(SWA_KV_HEADS), heads_t(SWA_KV_HEADS))
    consts = (pre_g, w_proj, w_vst, qn_g, wq, kvn_g, wk, wvt)
    assert len(ffn_consts) == N_FFN_CONSTS
    res = pl.pallas_call(
        _front_kernel,
        grid=(b, s // tm),
        in_specs=([row] + [_const_spec(c.shape) for c in ffn_consts] + [tab_spec] * len(tabs)
                  + [_const_spec(c.shape) for c in consts]),
        out_specs=[o[0] for o in outs],
        out_shape=[o[1] for o in outs],
        compiler_params=_params(2),
        name="front",
    )(x, *ffn_consts, *tabs, *consts)
    return res[0], res[1:]


LOG2_E =1.4426950408889634


def _mla_kernel(q_ref, k_ref, vt_ref, kl_ref, vtl_ref, o_ref, s_a, s_b, lead_a, lead_b, ot_scr,
                *, tq, ck):
    seq = k_ref.shape[2]
    n_chunks = seq // ck
    n_tiles = seq // tq
    sublanes = 8
    lead_ok = lax.broadcasted_iota(jnp.int32, (BLOCK, 1), 0) >= LEAD_PAD

    def fold_max(s):
        return jnp.max(s.reshape(s.shape[0] // sublanes, sublanes, tq), axis=0)

    def rows(t):
        return pl.ds(pl.multiple_of(t * tq, tq), tq)

    def score_chunk(j, q, c, s_new, mx):
        s = _dot_nt(k_ref[0, j, c * ck:(c + 1) * ck, :], q)
        s_new[c] = s
        return jnp.maximum(mx, fold_max(s))

    def value_chunk(j, c, s_old, m, acc):
        p = jnp.exp2(s_old[c] - m).astype(jnp.bfloat16)
        return acc + _dot(vt_ref[0, j, :, c * ck:(c + 1) * ck], p)

    def emit(j, t, acc):
        o_t = acc[:MLA_V] * (1.0 / acc[MLA_V:MLA_V + 1])
        if j % 2 == 0:
            ot_scr[t] = o_t
        else:
            both = jnp.concatenate([ot_scr[t], o_t], axis=0)
            slab = slice((j // 2) * LANES, (j // 2 + 1) * LANES)
            o_ref[0, rows(t), slab] = both.T.astype(o_ref.dtype)

    def lead_values(j, lead_old, mx_old):
        m = jnp.max(mx_old, axis=0, keepdims=True)
        return m, _dot(vtl_ref[0, j], jnp.exp2(lead_old[...] - m).astype(jnp.bfloat16))

    def stage(new, old, older, buf_new, buf_old, mx_old, acc_older):
        if older is not None:
            emit(*older, acc_older)
        j, t = new
        s_new, lead_new = buf_new
        q = q_ref[0, j, rows(t), :]
        s_lead = jnp.where(lead_ok, _dot_nt(kl_ref[0, j], q), NEG)
        lead_new[...] = s_lead
        mx, acc = fold_max(s_lead), None
        if old is not None:
            m, acc = lead_values(old[0], buf_old[1], mx_old)
        for c in range(n_chunks):
            mx = score_chunk(j, q, c, s_new, mx)
            if old is not None:
                acc = value_chunk(old[0], c, buf_old[0], m, acc)
        return mx, acc

    buf_a, buf_b = (s_a, lead_a), (s_b, lead_b)
    n_heads = q_ref.shape[1]
    end = n_tiles - 1
    for j in range(n_heads):
        if j == 0:
            mx, _ = stage((0, 0), None, None, buf_a, None, None, None)
            mx, acc = stage((0, 1), (0, 0), None, buf_b, buf_a, mx, None)
        else:
            mx, acc = stage((j, 0), (j - 1, end), (j - 1, end - 1), buf_a, buf_b, mx, acc)
            mx, acc = stage((j, 1), (j, 0), (j - 1, end), buf_b, buf_a, mx, acc)

        def pair(i, carry, j=j):
            t = 2 * i + 2
            mx, acc = stage((j, t), (j, t - 1), (j, t - 2), buf_a, buf_b, *carry)
            return stage((j, t + 1), (j, t), (j, t - 1), buf_b, buf_a, mx, acc)

        mx, acc = lax.fori_loop(0, n_tiles // 2 - 1, pair, (mx, acc))
    last = n_heads - 1
    emit(last, end - 1, acc)
    m, acc = lead_values(last, lead_b, mx)
    for c in range(n_chunks):
        acc = value_chunk(last, c, s_b, m, acc)
    emit(last, end, acc)


def _mla_heads_per_step(nh, s, tq, ck):
    bf16_bytes, f32_bytes, pipeline_buffers = 2, 4, 2
    scratch = 2 * (s + BLOCK) * tq * f32_bytes + s * MLA_V * f32_bytes
    in_flight = 2 * ck * tq * f32_bytes
    for hp in range(nh, 0, -2):
        per_head = (3 * s * LANES + s * MLA_V) * bf16_bytes
        if nh % hp == 0 and pipeline_buffers * hp * per_head + scratch + in_flight <= VMEM_LIMIT:
            return hp
    raise ValueError("sequence too long for the resident-key MLA kernel")


def _mla(q, k, vt, k_lead, vt_lead, tq, ck):
    b, nh, s, _ = q.shape
    hp = _mla_heads_per_step(nh, s, tq, ck)
    assert s % ck == 0 and s % (2 * tq) == 0
    seq_spec = pl.BlockSpec((1, hp, s, LANES), lambda i, p: (i, p, 0, 0))
    vt_spec = pl.BlockSpec((1, hp, LANES, s), lambda i, p: (i, p, 0, 0))
    lead_spec = pl.BlockSpec((1, hp, BLOCK, LANES), lambda i, p: (0, p, 0, 0))
    score_buf = pltpu.VMEM((s // ck, ck, tq), jnp.float32)
    lead_buf = pltpu.VMEM((BLOCK, tq), jnp.float32)
    return pl.pallas_call(
        functools.partial(_mla_kernel, tq=tq, ck=ck),
        grid=(b, nh // hp),
        in_specs=[seq_spec, seq_spec, vt_spec, lead_spec, lead_spec],
        out_specs=pl.BlockSpec((1, s, hp * MLA_V), lambda i, p: (i, 0, p)),
        out_shape=jax.ShapeDtypeStruct((b, s, nh * MLA_V), jnp.bfloat16),
        scratch_shapes=[score_buf, score_buf, lead_buf, lead_buf,
                        pltpu.VMEM((s // tq, MLA_V, tq), jnp.float32)],
        compiler_params=_params(2),
        name="mla",
    )(q, k, vt, k_lead, vt_lead)


def _window_kernel(sink_ref, q_ref, kl_ref, kc_ref, kr_ref, km_ref,
                   vl_ref, vc_ref, vr_ref, vm_ref, o_ref):
    tq = q_ref.shape[2]
    n_blk = tq // BLOCK
    t = pl.program_id(1)
    n_t = pl.num_programs(1)
    n_keys = 4 * BLOCK
    n_cols = SWA_GROUP * BLOCK
    key = lax.broadcasted_iota(jnp.int32, (n_keys, n_cols), 0)
    query = lax.broadcasted_iota(jnp.int32, (n_keys, n_cols), 1) % BLOCK
    in_left = jnp.logical_and(key < BLOCK, key >= query)
    in_right = jnp.logical_and(jnp.logical_and(key >= 2 * BLOCK, key < 3 * BLOCK),
                               key - 2 * BLOCK <= query)
    in_rest = jnp.logical_or(jnp.logical_and(key >= BLOCK, key < 2 * BLOCK),
                             key >= 3 * BLOCK + LEAD_PAD)
    sublanes = 8

    def keys_of(blk, left_ref, center_ref, right_ref, lead_ref, g):
        first, last = blk == 0, blk == n_blk - 1
        return jnp.concatenate(
            [left_ref[0, g] if first else center_ref[0, g, (blk - 1) * BLOCK:blk * BLOCK, :],
             center_ref[0, g, blk * BLOCK:(blk + 1) * BLOCK, :],
             right_ref[0, g] if last else center_ref[0, g, (blk + 1) * BLOCK:(blk + 2) * BLOCK, :],
             lead_ref[0, g]], axis=0)

    def values_t_of(blk, g):
        first, last = blk == 0, blk == n_blk - 1
        return jnp.concatenate(
            [vl_ref[0, g] if first else vc_ref[0, g, :, (blk - 1) * BLOCK:blk * BLOCK],
             vc_ref[0, g, :, blk * BLOCK:(blk + 1) * BLOCK],
             vr_ref[0, g] if last else vc_ref[0, g, :, (blk + 1) * BLOCK:(blk + 2) * BLOCK],
             vm_ref[0, g]], axis=1)

    def scores(g, blk):
        ok_left = jnp.logical_and(in_left, t > 0) if blk == 0 else in_left
        ok_right = jnp.logical_and(in_right, t < n_t - 1) if blk == n_blk - 1 else in_right
        ok = jnp.logical_or(jnp.logical_or(ok_left, ok_right), in_rest)
        q = q_ref[0, g * SWA_GROUP:(g + 1) * SWA_GROUP, blk * BLOCK:(blk + 1) * BLOCK, :]
        s = _dot_nt(keys_of(blk, kl_ref, kc_ref, kr_ref, km_ref, g),
                    q.reshape(n_cols, LANES))
        return jnp.where(ok, s, NEG)

    def finish(g, blk, s):
        sink = jnp.concatenate(
            [jnp.full((1, BLOCK), sink_ref[g * SWA_GROUP + r] * LOG2_E, jnp.float32)
             for r in range(SWA_GROUP)], axis=1)
        folded = jnp.max(s.reshape(n_keys // sublanes, sublanes, n_cols), axis=0)
        m = jnp.maximum(jnp.max(folded, axis=0, keepdims=True), sink)
        p = jnp.exp2(s - m).astype(jnp.bfloat16)
        acc = _dot(values_t_of(blk, g), p)
        o_t = acc[:SWA_HEAD_DIM] * (
            1.0 / (acc[SWA_HEAD_DIM:SWA_HEAD_DIM + 1] + jnp.exp2(sink - m)))
        for pair in range(SWA_GROUP // 2):
            both = jnp.concatenate(
                [o_t[:, (2 * pair) * BLOCK:(2 * pair + 1) * BLOCK],
                 o_t[:, (2 * pair + 1) * BLOCK:(2 * pair + 2) * BLOCK]], axis=0)
            slab = g * (SWA_GROUP // 2) + pair
            o_ref[0, blk * BLOCK:(blk + 1) * BLOCK, slab * LANES:(slab + 1) * LANES] = (
                both.T.astype(o_ref.dtype))

    units = [(g, blk) for g in range(SWA_KV_HEADS) for blk in range(n_blk)]
    ahead = 3
    pending = [scores(*u) for u in units[:ahead]]
    for i, unit in enumerate(units):
        if i + ahead < len(units):
            pending.append(scores(*units[i + ahead]))
        finish(*unit, pending.pop(0))


def _window(sink, q, k, v, k_lead, v_lead, tq):
    b, nh, s, _ = q.shape
    n_blk = s // BLOCK
    per = tq // BLOCK

    def specs(n, transposed):
        def spec(rows, index):
            if transposed:
                return pl.BlockSpec((1, n, LANES, rows), lambda i, t: (*index(i, t)[:2], 0,
                                                                       index(i, t)[2]))
            return pl.BlockSpec((1, n, rows, LANES), lambda i, t: (*index(i, t), 0))

        return [spec(BLOCK, lambda i, t: (i, 0, jnp.maximum(t * per - 1, 0))),
                spec(tq, lambda i, t: (i, 0, t)),
                spec(BLOCK, lambda i, t: (i, 0, jnp.minimum((t + 1) * per, n_blk - 1))),
                spec(BLOCK, lambda i, t: (0, 0, 0))]

    smem = pl.BlockSpec(memory_space=pltpu.SMEM)
    return pl.pallas_call(
        _window_kernel,
        grid=(b, s // tq),
        in_specs=([smem, specs(nh, False)[1]] + specs(SWA_KV_HEADS, False)
                  + specs(SWA_KV_HEADS, True)),
        out_specs=pl.BlockSpec((1, tq, nh * SWA_HEAD_DIM), lambda i, t: (i, t, 0)),
        out_shape=jax.ShapeDtypeStruct((b, s, nh * SWA_HEAD_DIM), jnp.bfloat16),
        compiler_params=_params(2),
        name="window",
    )(sink, q, k, k, k, k_lead, v, v, v, v_lead)


def _mixout_kernel(x_ref, oa_ref, ob_ref, pre_g_ref, wg_ref, woa_ref, wob_ref, wout_ref,
                   post_g_ref, o_ref):
    half = x_ref.shape[1] // 2
    for rows in (slice(0, half), slice(half, 2 * half)):
        x1 = x_ref[0, rows, :]
        h = _rms(x1, pre_g_ref[...]).astype(jnp.bfloat16)
        y_a = _dot(oa_ref[0, rows, :], woa_ref[...])
        y_b = _dot(ob_ref[0, rows, :], wob_ref[...])
        merged = (jax.nn.sigmoid(_dot(h, wg_ref[:, :D_MODEL])) * y_a
                  + jax.nn.sigmoid(_dot(h, wg_ref[:, D_MODEL:])) * y_b)
        o_ref[0, rows, :] = x1 + _rms(_dot(merged.astype(jnp.bfloat16), wout_ref[...]),
                                      post_g_ref[...])


def _mixout(x1, o_a, o_b, pre_g, wg, woa, wob, wout, post_g, tm):
    b, s, d = x1.shape
    row = pl.BlockSpec((1, tm, d), lambda i, t: (i, t, 0))
    half = pl.BlockSpec((1, tm, o_a.shape[2]), lambda i, t: (i, t, 0))
    return pl.pallas_call(
        _mixout_kernel,
        grid=(b, s // tm),
        in_specs=[row, half, half, _const_spec(pre_g.shape), _const_spec(wg.shape),
                  _const_spec(woa.shape), _const_spec(wob.shape), _const_spec(wout.shape),
                  _const_spec(post_g.shape)],
        out_specs=row,
        out_shape=jax.ShapeDtypeStruct(x1.shape, jnp.float32),
        compiler_params=_params(2),
        name="mixout",
    )(x1, o_a, o_b, pre_g, wg, woa, wob, wout, post_g)


N_MIXOUT_INPUTS = 9


def _back_kernel(*refs):
    o_ref = refs[-1]
    _mixout_kernel(*refs[:N_MIXOUT_INPUTS], o_ref)
    _ffn_kernel(o_ref, *refs[N_MIXOUT_INPUTS:-1], o_ref)


def _back(x1, o_a, o_b, mix_consts, ffn_consts, tm):
    b, s, d = x1.shape
    row = pl.BlockSpec((1, tm, d), lambda i, t: (i, t, 0))
    half = pl.BlockSpec((1, tm, o_a.shape[2]), lambda i, t: (i, t, 0))
    consts = tuple(mix_consts) + tuple(ffn_consts)
    assert 3 + len(mix_consts) == N_MIXOUT_INPUTS and len(ffn_consts) == N_FFN_CONSTS
    return pl.pallas_call(
        _back_kernel,
        grid=(b, s // tm),
        in_specs=[row, half, half] + [_const_spec(c.shape) for c in consts],
        out_specs=row,
        out_shape=jax.ShapeDtypeStruct(x1.shape, jnp.float32),
        compiler_params=_params(2),
        name="back",
    )(x1, o_a, o_b, *consts)


def _pack_weights(w_in, w_uq, w_ukv):
    bf16 = jnp.bfloat16
    d = w_in.shape[0]
    o = 0
    parts = []
    for n in (MLA_Q_LORA, MLA_KV_LORA, MLA_ROPE, SWA_HEADS * SWA_HEAD_DIM,
              SWA_KV_HEADS * SWA_HEAD_DIM, SWA_KV_HEADS * SWA_HEAD_DIM, D_MODEL, D_MODEL):
        parts.append(w_in[:, o:o + n])
        o += n
    w_cq, w_ckv, w_kr, w_qs, w_ks, w_vs, w_ga, w_gb = parts

    def mla_slab(nope, rope):
        half = MLA_ROPE // 2

        def interleave(x, y):
            return jnp.stack([x, y], axis=-1).reshape(x.shape[:-1] + (2 * half,))

        tail = jnp.zeros(nope.shape[:-1] + (LANES - MLA_NOPE - MLA_ROPE,), nope.dtype)
        return jnp.concatenate(
            [interleave(rope[..., :half], nope[..., :half]), nope[..., half:MLA_NOPE - half],
             interleave(rope[..., half:], nope[..., MLA_NOPE - half:]), tail], axis=-1)

    def window_pairs(w):
        quarter = SWA_HEAD_DIM // 2
        w = w.reshape(d, -1, 2, 2, quarter)
        return jnp.swapaxes(w, 2, 3).reshape(d, -1)

    kr_slab = mla_slab(jnp.zeros((d, MLA_NOPE), w_kr.dtype), w_kr)
    w_proj = jnp.concatenate([w_cq, w_ckv, window_pairs(w_qs), window_pairs(w_ks), kr_slab],
                             axis=1).astype(bf16)
    w_gate = jnp.concatenate([w_ga, w_gb], axis=1).astype(bf16)
    w_vst = w_vs.T.astype(bf16)

    r = w_uq.shape[0]
    uq = w_uq.reshape(r, MLA_HEADS, MLA_NOPE + MLA_ROPE)
    wq = mla_slab(uq[..., :MLA_NOPE], uq[..., MLA_NOPE:]).reshape(r, -1).astype(bf16)

    r = w_ukv.shape[0]
    ukv = w_ukv.reshape(r, MLA_HEADS, MLA_NOPE + MLA_V)
    no_rope = jnp.zeros((r, MLA_HEADS, MLA_ROPE), w_ukv.dtype)
    wk = mla_slab(ukv[..., :MLA_NOPE], no_rope).reshape(r, -1).astype(bf16)
    zeros = jnp.zeros((r, MLA_HEADS, LANES - MLA_V), w_ukv.dtype)
    v_slab = jnp.concatenate([ukv[..., MLA_NOPE:], zeros], axis=-1)
    wvt = v_slab.reshape(r, -1).T.astype(bf16)
    return w_proj, w_gate, wq, wk, wvt, w_vst


def _rope_tables(pos):
    pos = pos.astype(jnp.float32)[:, None]
    lane = jnp.arange(LANES)[None, :]
    half = SWA_HEAD_DIM // 2
    inv = ROPE_THETA ** (-jnp.arange(half, dtype=jnp.float32) / half)
    ang = pos * inv[lane % half]
    return jnp.cos(ang), jnp.where(lane < ROPE_PARTNER, -jnp.sin(ang), jnp.sin(ang))


def _row_tile(s):
    return min(s, 512)


def kernel(x_prompt, x_sample, meta_tokens, ffn1_pre_g, ffn1_w_in, ffn1_w_out, ffn1_post_g,
           mix_pre_g, w_in, q_norm_g, w_uq, kv_norm_g, w_ukv, sink, w_o_a, w_o_b, w_out,
           mix_post_g, ffn2_pre_g, ffn2_w_in, ffn2_w_out, ffn2_post_g):
    assert ffn1_w_in.shape[0] == 1, "single layer"
    bf16 = jnp.bfloat16
    f1 = (ffn1_pre_g, ffn1_w_in[0].astype(bf16), ffn1_w_out[0].astype(bf16), ffn1_post_g)
    f2 = (ffn2_pre_g, ffn2_w_in[0].astype(bf16), ffn2_w_out[0].astype(bf16), ffn2_post_g)
    w_proj, w_gate, wq, wk, wvt, w_vst = _pack_weights(w_in[0], w_uq[0], w_ukv[0])
    woa, wob, wout = w_o_a[0].astype(bf16), w_o_b[0].astype(bf16), w_out[0].astype(bf16)
    sink = sink[0]

    def front(x, tabs):
        tm = _row_tile(x.shape[1])
        return _front(x, f1, tabs, mix_pre_g, w_proj, w_vst, q_norm_g, wq, kv_norm_g, wk, wvt,
                      tm)

    lead = jnp.concatenate([jnp.zeros((LEAD_PAD, D_MODEL), x_prompt.dtype),
                            meta_tokens.astype(x_prompt.dtype)], axis=0)[None]
    _, (_, ka_lead, va_lead, _, kw_lead, vw_lead) = front(
        lead, _rope_tables(jnp.arange(BLOCK) - LEAD_PAD))
    seq_tabs = _rope_tables(jnp.arange(max(x_prompt.shape[1], x_sample.shape[1])) + N_META)

    def trunk(x):
        s = x.shape[1]
        tm = _row_tile(s)
        x1, (qa, ka, va, qw, kw, vw) = front(x, seq_tabs)
        o_a = _mla(qa, ka, va, ka_lead, va_lead, tq=MXU_COLS, ck=1024)
        o_b = _window(sink, qw, kw, vw, kw_lead, vw_lead, tq=8 * BLOCK)
        return _back(x1, o_a, o_b, (mix_pre_g, w_gate, woa, wob, wout, mix_post_g), f2, tm)

    return (trunk(x_prompt), trunk(x_sample))
```
